```python
import math
import jax, jax.numpy as jnp
from jax import lax
import numpy as np

D_MODEL = 2048
BATCH = 1
SEQ = 8192
DEPTH = 2
DEC_BATCH = 32
DEC_SEQ = 1
PAST_LEN = 8192
PAGE_SIZE = 128

DH_DIFF = 64
DV_DIFF = 2 * DH_DIFF
W_DIFF = D_MODEL // 2
N_HEADS_DIFF = W_DIFF // DV_DIFF
DH_FOX = 128
W_FOX = D_MODEL - W_DIFF
N_HEADS_FOX = W_FOX // DH_FOX
FORGET_BIAS = 4.0
IN_SPLITS = [W_DIFF, 2 * W_DIFF, 3 * W_DIFF, 3 * W_DIFF + W_FOX, 3 * W_DIFF + 2 * W_FOX, 3 * W_DIFF + 3 * W_FOX]
IN_WIDTH = 3 * W_DIFF + 3 * W_FOX + N_HEADS_FOX
N_MEM = 256
N_HEADS_MEM = 4
DH_MEM = 128
W_MEM = N_HEADS_MEM * DH_MEM
D_FF = -(-8 * D_MODEL // (3 * 256)) * 256
N_BUCKETS = 32
MAX_DISTANCE = 128
Q_BLOCK = 128
ALPHA = (2 * DEPTH) ** 0.25
BETA = (8 * DEPTH) ** -0.25
LN_EPS = 1e-5

kernel_name = "hymba_diff_fox_deepnorm_decode_step"


def lambda_init(layer):
    return 0.8 - 0.6 * math.exp(-0.3 * layer)


def rel_bucket(dist):
    n = jnp.maximum(dist, 0)
    max_exact = N_BUCKETS // 2
    nf = jnp.maximum(n, 1).astype(jnp.float32)
    large = max_exact + (jnp.log(nf / max_exact) / math.log(MAX_DISTANCE / max_exact)
                         * (N_BUCKETS - max_exact)).astype(jnp.int32)
    large = jnp.minimum(large, N_BUCKETS - 1)
    return jnp.where(n < max_exact, n, large)


def post_norm(x, y, g, b):
    h = (ALPHA * x + y).astype(jnp.float32)
    mu = jnp.mean(h, axis=-1, keepdims=True)
    var = jnp.mean(jnp.square(h - mu), axis=-1, keepdims=True)
    return ((h - mu) * lax.rsqrt(var + LN_EPS) * g + b).astype(x.dtype)


def mixer_inputs(x, w_in, b_f):
    B, T, _ = x.shape
    z = jnp.einsum('btd,de->bte', x, w_in)
    dq, dk, dv, fq, fk, fv, fl = jnp.split(z, IN_SPLITS, axis=-1)
    dq = dq.reshape(B, T, N_HEADS_DIFF, 2, DH_DIFF)
    dk = dk.reshape(B, T, N_HEADS_DIFF, 2, DH_DIFF)
    dv = dv.reshape(B, T, N_HEADS_DIFF, DV_DIFF)
    fq = fq.reshape(B, T, N_HEADS_FOX, DH_FOX)
    fk = fk.reshape(B, T, N_HEADS_FOX, DH_FOX)
    fv = fv.reshape(B, T, N_HEADS_FOX, DH_FOX)
    logf = jax.nn.log_sigmoid((fl + b_f).astype(jnp.float32))
    return dq, dk, dv, fq, fk, fv, logf


def diff_lambda_value(lp, lam_init):
    lp = lp.astype(jnp.float32)
    return jnp.exp(jnp.sum(lp[0] * lp[1])) - jnp.exp(jnp.sum(lp[2] * lp[3])) + lam_init


def diff_attention(q, k, v, q_pos, k_pos, rel_bias, lam, subln_g, lam_init):
    dist = q_pos[:, None] - k_pos[None, :]
    bias = jnp.transpose(rel_bias[rel_bucket(dist)], (2, 0, 1)).astype(jnp.float32)
    s = jnp.einsum('bqhmd,bkhmd->bmhqk', q, k).astype(jnp.float32) * (DH_DIFF ** -0.5) + bias
    s = jnp.where(dist >= 0, s, -jnp.inf)
    p = jax.nn.softmax(s, axis=-1)
    a = p[:, 0] - lam * p[:, 1]
    o = jnp.einsum('bhqk,bkhe->bqhe', a.astype(v.dtype), v).astype(jnp.float32)
    o = o * lax.rsqrt(jnp.mean(jnp.square(o), axis=-1, keepdims=True) + LN_EPS)
    return (o * subln_g * (1.0 - lam_init)).astype(v.dtype)


def fox_attention(q, k, v, cq, ck, q_pos, k_pos):
    dist = q_pos[:, None] - k_pos[None, :]
    decay = jnp.transpose(cq, (0, 2, 1))[:, :, :, None] - jnp.transpose(ck, (0, 2, 1))[:, :, None, :]
    s = jnp.einsum('bqhd,bkhd->bhqk', q, k).astype(jnp.float32) * (DH_FOX ** -0.5) + decay
    s = jnp.where(dist >= 0, s, -jnp.inf)
    p = jax.nn.softmax(s, axis=-1)
    return jnp.einsum('bhqk,bkhd->bqhd', p.astype(v.dtype), v)


def mixer_output(od, of, w_o):
    B, T = od.shape[:2]
    h = jnp.concatenate([od.reshape(B, T, W_DIFF), of.reshape(B, T, W_FOX)], axis=-1)
    return jnp.einsum('bte,ed->btd', h, w_o)


def mem_kv(mem, w_mkv):
    B = mem.shape[0]
    mk, mv = jnp.split(jnp.einsum('bmd,de->bme', mem, w_mkv), 2, axis=-1)
    return (mk.reshape(B, N_MEM, N_HEADS_MEM, DH_MEM), mv.reshape(B, N_MEM, N_HEADS_MEM, DH_MEM))


def mem_attention(x, mk, mv, w_mq, w_mo):
    B, T, _ = x.shape
    q = jnp.einsum('btd,de->bte', x, w_mq).reshape(B, T, N_HEADS_MEM, DH_MEM)
    s = jnp.einsum('bqhd,bkhd->bhqk', q, mk).astype(jnp.float32) * (DH_MEM ** -0.5)
    p = jax.nn.softmax(s, axis=-1)
    o = jnp.einsum('bhqk,bkhd->bqhd', p.astype(mv.dtype), mv).reshape(B, T, W_MEM)
    return jnp.einsum('bte,ed->btd', o, w_mo)


def swiglu(x, w_ff_in, w_ff_out):
    g, u = jnp.split(jnp.einsum('btd,df->btf', x, w_ff_in), 2, axis=-1)
    return jnp.einsum('btf,fd->btd', jax.nn.silu(g) * u, w_ff_out)


def to_blocks(a, nb):
    B, T = a.shape[:2]
    return jnp.moveaxis(a.reshape(B, nb, T // nb, *a.shape[2:]), 1, 0)


def from_blocks(a):
    nb, B, Q = a.shape[:3]
    return jnp.moveaxis(a, 0, 1).reshape(B, nb * Q, *a.shape[3:])


def gather_pages(pool, page_table):
    g = pool[page_table]
    DB, NP, PS = g.shape[:3]
    return g.reshape(DB, NP * PS, *g.shape[3:])


def setup_inputs(seed: int = 0) -> dict:
    key = jax.random.key(seed)
    ks = jax.random.split(key, 24)
    f32 = jnp.float32
    n_pages = PAST_LEN // PAGE_SIZE
    n_used = DEC_BATCH * n_pages
    pool = n_used + n_used // 4

    def nrm(k, shape, scale=1.0):
        return scale * jax.random.normal(k, shape, f32)

    return {
        "x_prompt": nrm(ks[0], (BATCH, SEQ, D_MODEL)),
        "x_sample": nrm(ks[1], (DEC_BATCH, DEC_SEQ, D_MODEL)),
        "cache_diff_k": nrm(ks[2], (DEPTH, pool, PAGE_SIZE, N_HEADS_DIFF, DV_DIFF)),
        "cache_diff_v": nrm(ks[3], (DEPTH, pool, PAGE_SIZE, N_HEADS_DIFF, DV_DIFF)),
        "cache_fox_k": nrm(ks[4], (DEPTH, pool, PAGE_SIZE, N_HEADS_FOX, DH_FOX)),
        "cache_fox_v": nrm(ks[5], (DEPTH, pool, PAGE_SIZE, N_HEADS_FOX, DH_FOX)),
        "cache_fox_logf": jax.nn.log_sigmoid(FORGET_BIAS + nrm(ks[6], (DEPTH, pool, PAGE_SIZE, N_HEADS_FOX))),
        "cache_mem_k": nrm(ks[7], (DEPTH, DEC_BATCH, N_MEM, N_HEADS_MEM, DH_MEM)),
        "cache_mem_v": nrm(ks[8], (DEPTH, DEC_BATCH, N_MEM, N_HEADS_MEM, DH_MEM)),
        "page_table": jax.random.permutation(ks[9], pool)[:n_used].reshape(DEC_BATCH, n_pages).astype(jnp.int32),
        "mem_prompt": nrm(ks[10], (BATCH, N_MEM, D_MODEL)),
        "w_in": nrm(ks[11], (DEPTH, D_MODEL, IN_WIDTH), D_MODEL ** -0.5),
        "b_forget": FORGET_BIAS + nrm(ks[12], (DEPTH, N_HEADS_FOX), 0.1),
        "diff_lambda": nrm(ks[13], (DEPTH, 4, DH_DIFF), 0.1),
        "diff_subln_g": 1.0 + nrm(ks[14], (DEPTH, DV_DIFF), 0.02),
        "rel_bias": nrm(ks[15], (N_BUCKETS, N_HEADS_DIFF), 0.5),
        "w_o": nrm(ks[16], (DEPTH, D_MODEL, D_MODEL), BETA * D_MODEL ** -0.5),
        "w_mq": nrm(ks[17], (DEPTH, D_MODEL, W_MEM), D_MODEL ** -0.5),
        "w_mkv": nrm(ks[18], (DEPTH, D_MODEL, 2 * W_MEM), D_MODEL ** -0.5),
        "w_mo": nrm(ks[19], (DEPTH, W_MEM, D_MODEL), BETA * W_MEM ** -0.5),
        "w_ff_in": nrm(ks[20], (DEPTH, D_MODEL, 2 * D_FF), D_MODEL ** -0.5),
        "w_ff_out": nrm(ks[21], (DEPTH, D_FF, D_MODEL), BETA * D_FF ** -0.5),
        "ln_g": 1.0 + nrm(ks[22], (DEPTH, 3, D_MODEL), 0.02),
        "ln_b": nrm(ks[23], (DEPTH, 3, D_MODEL), 0.02),
    }


def reference(x_prompt, x_sample, cache_diff_k, cache_diff_v, cache_fox_k, cache_fox_v, cache_fox_logf,
              cache_mem_k, cache_mem_v, page_table, mem_prompt, w_in, b_forget, diff_lambda, diff_subln_g,
              rel_bias, w_o, w_mq, w_mkv, w_mo, w_ff_in, w_ff_out, ln_g, ln_b):
    B, T, _ = x_prompt.shape
    nb = T // Q_BLOCK
    pos = jnp.arange(T, dtype=jnp.int32)
    x = x_prompt
    p_dk, p_dv, p_fk, p_fv, p_lf, p_mk, p_mv = [], [], [], [], [], [], []
    for l in range(DEPTH):
        lam0 = lambda_init(l)
        lam = diff_lambda_value(diff_lambda[l], lam0)
        dq, dk, dv, fq, fk, fv, logf = mixer_inputs(x, w_in[l], b_forget[l])
        c = jnp.cumsum(logf, axis=1)

        def block(args, dk=dk, dv=dv, fk=fk, fv=fv, c=c, lam=lam, lam0=lam0, l=l):
            qd, qf, cq, qpos = args
            od = diff_attention(qd, dk, dv, qpos, pos, rel_bias, lam, diff_subln_g[l], lam0)
            of = fox_attention(qf, fk, fv, cq, c, qpos, pos)
            return od, of

        od, of = lax.map(block, (to_blocks(dq, nb), to_blocks(fq, nb), to_blocks(c, nb),
                                 pos.reshape(nb, Q_BLOCK)))
        x = post_norm(x, mixer_output(from_blocks(od), from_blocks(of), w_o[l]), ln_g[l, 0], ln_b[l, 0])
        mk, mv = mem_kv(mem_prompt, w_mkv[l])
        x = post_norm(x, mem_attention(x, mk, mv, w_mq[l], w_mo[l]), ln_g[l, 1], ln_b[l, 1])
        x = post_norm(x, swiglu(x, w_ff_in[l], w_ff_out[l]), ln_g[l, 2], ln_b[l, 2])
        p_dk.append(dk.reshape(B, T, N_HEADS_DIFF, DV_DIFF))
        p_dv.append(dv)
        p_fk.append(fk)
        p_fv.append(fv)
        p_lf.append(logf)
        p_mk.append(mk)
        p_mv.append(mv)
    y_prompt = x

    DB, DS, _ = x_sample.shape
    P = page_table.shape[1] * PAGE_SIZE
    q_pos = P + jnp.arange(DS, dtype=jnp.int32)
    k_pos = jnp.arange(P + DS, dtype=jnp.int32)
    x = x_sample
    s_dk, s_dv, s_fk, s_fv, s_lf = [], [], [], [], []
    for l in range(DEPTH):
        lam0 = lambda_init(l)
        lam = diff_lambda_value(diff_lambda[l], lam0)
        dq, dk, dv, fq, fk, fv, logf = mixer_inputs(x, w_in[l], b_forget[l])
        dk_all = jnp.concatenate([gather_pages(cache_diff_k[l], page_table).reshape(DB, P, N_HEADS_DIFF, 2, DH_DIFF), dk], axis=1)
        dv_all = jnp.concatenate([gather_pages(cache_diff_v[l], page_table), dv], axis=1)
        fk_all = jnp.concatenate([gather_pages(cache_fox_k[l], page_table), fk], axis=1)
        fv_all = jnp.concatenate([gather_pages(cache_fox_v[l], page_table), fv], axis=1)
        lf_all = jnp.concatenate([gather_pages(cache_fox_logf[l], page_table).astype(jnp.float32), logf], axis=1)
        c_all = jnp.cumsum(lf_all, axis=1)
        od = diff_attention(dq, dk_all, dv_all, q_pos, k_pos, rel_bias, lam, diff_subln_g[l], lam0)
        of = fox_attention(fq, fk_all, fv_all, c_all[:, P:], c_all, q_pos, k_pos)
        x = post_norm(x, mixer_output(od, of, w_o[l]), ln_g[l, 0], ln_b[l, 0])
        x = post_norm(x, mem_attention(x, cache_mem_k[l], cache_mem_v[l], w_mq[l], w_mo[l]), ln_g[l, 1], ln_b[l, 1])
        x = post_norm(x, swiglu(x, w_ff_in[l], w_ff_out[l]), ln_g[l, 2], ln_b[l, 2])
        s_dk.append(dk.reshape(DB, DS, N_HEADS_DIFF, DV_DIFF))
        s_dv.append(dv)
        s_fk.append(fk)
        s_fv.append(fv)
        s_lf.append(logf)
    y_sample = x

    return (y_prompt, y_sample,
            jnp.stack(p_dk), jnp.stack(p_dv), jnp.stack(p_fk), jnp.stack(p_fv), jnp.stack(p_lf),
            jnp.stack(p_mk), jnp.stack(p_mv),
            jnp.stack(s_dk), jnp.stack(s_dv), jnp.stack(s_fk), jnp.stack(s_fv), jnp.stack(s_lf))
```

```python
import functools
import math

import jax
import jax.numpy as jnp
from jax import lax
from jax.experimental import pallas as pl
from jax.experimental.pallas import tpu as pltpu

F32 = jnp.float32
BF16 = jnp.bfloat16

LANES = 128
NEG = -1e30
LN_EPS = 1e-5
N_BUCKETS = 32
MAX_DISTANCE = 128
DH_DIFF = 64
HEAD_W = 128
N_HEADS = 8
N_HEADS_MEM = 4
ATT_BLK = 512
VMEM_LIMIT = 56 * 1024 * 1024

NT_DIMS = (((1,), (1,)), ((), ()))


def _cparams(sem):
    return pltpu.CompilerParams(dimension_semantics=sem, vmem_limit_bytes=VMEM_LIMIT)


def _lambda_init(layer):
    return 0.8 - 0.6 * math.exp(-0.3 * layer)


def _rel_bucket(dist):
    n = jnp.maximum(dist, 0)
    max_exact = N_BUCKETS // 2
    nf = jnp.maximum(n, 1).astype(F32)
    large = max_exact + (jnp.log(nf / max_exact) / math.log(MAX_DISTANCE / max_exact)
                         * (N_BUCKETS - max_exact)).astype(jnp.int32)
    large = jnp.minimum(large, N_BUCKETS - 1)
    return jnp.where(n < max_exact, n, large)


def _split3(x):
    hi = x.astype(BF16)
    r1 = x - hi.astype(F32)
    mid = r1.astype(BF16)
    lo = (r1 - mid.astype(F32)).astype(BF16)
    return hi, mid, lo


def _lane_cat(a):
    return jnp.concatenate([a[h] for h in range(a.shape[0])], axis=1)


def _post_norm(h, g, b):
    mu = jnp.mean(h, axis=-1, keepdims=True)
    d = h - mu
    var = jnp.mean(d * d, axis=-1, keepdims=True)
    return d * lax.rsqrt(var + LN_EPS) * g + b


def _lam_value(lp, lam0):
    a = jnp.sum(lp[0:1] * lp[1:2], axis=-1, keepdims=True)
    b = jnp.sum(lp[2:3] * lp[3:4], axis=-1, keepdims=True)
    return jnp.exp(a) - jnp.exp(b) + lam0


def _proj_kernel(x_ref, w_ref, *out_refs, scales, want_f32, b16_layout):
    j = pl.program_id(0)
    acc = jnp.dot(x_ref[...], w_ref[...], preferred_element_type=F32)
    k = 0
    if want_f32:
        out_refs[k][...] = acc
        k += 1
    if b16_layout is not None:
        scale = jnp.float32(scales[-1])
        for jj in range(len(scales) - 2, -1, -1):
            scale = jnp.where(j == jj, jnp.float32(scales[jj]), scale)
        y = (acc * scale).astype(BF16)
        o = out_refs[k]
        if b16_layout == "heads":
            for hh in range(o.shape[0]):
                o[hh] = y[:, hh * LANES:(hh + 1) * LANES]
        else:
            o[...] = y


def _proj(x, w3, l, n_cols, tn, tm, scales=None, want_f32=False, b16_layout="flat"):
    M, K = x.shape
    nj, ni = n_cols // tn, M // tm
    scales = tuple(scales) if scales is not None else (1.0,) * nj
    out_shape, out_specs = [], []
    if want_f32:
        out_shape.append(jax.ShapeDtypeStruct((M, n_cols), F32))
        out_specs.append(pl.BlockSpec((tm, tn), lambda j, i: (i, j)))
    if b16_layout == "heads":
        out_shape.append(jax.ShapeDtypeStruct((n_cols // LANES, M, LANES), BF16))
        out_specs.append(pl.BlockSpec((tn // LANES, tm, LANES), lambda j, i: (j, i, 0)))
    elif b16_layout == "flat":
        out_shape.append(jax.ShapeDtypeStruct((M, n_cols), BF16))
        out_specs.append(pl.BlockSpec((tm, tn), lambda j, i: (i, j)))
    return pl.pallas_call(
        functools.partial(_proj_kernel, scales=scales, want_f32=want_f32, b16_layout=b16_layout),
        grid=(nj, ni),
        in_specs=[pl.BlockSpec((tm, K), lambda j, i: (i, 0)),
                  pl.BlockSpec((None, K, tn), lambda j, i: (l, 0, j))],
        out_specs=out_specs, out_shape=out_shape,
        compiler_params=_cparams(("arbitrary", "arbitrary")),
        name="proj",
    )(x, w3)


def _logit_kernel(x_ref, w_ref, b_ref, lf_ref, *rest, n_valid, with_cumsum):
    fl = jnp.dot(x_ref[...], w_ref[...], preferred_element_type=F32) + b_ref[...]
    lf = jnp.minimum(fl, 0.0) - jnp.log1p(jnp.exp(-jnp.abs(fl)))
    tm = lf.shape[0]
    lane = lax.broadcasted_iota(jnp.int32, lf.shape, 1)
    lf = jnp.where(lane < n_valid, lf, 0.0)
    lf_ref[...] = lf
    if with_cumsum:
        c_ref, ct_ref, carry = rest

        @pl.when(pl.program_id(0) == 0)
        def _():
            carry[...] = jnp.zeros_like(carry)

        row = lax.broadcasted_iota(jnp.int32, (tm, tm), 0)
        col = lax.broadcasted_iota(jnp.int32, (tm, tm), 1)
        tri = (row >= col).astype(BF16)
        hi, mid, lo = _split3(lf)
        c = (jnp.dot(tri, hi, preferred_element_type=F32)
             + jnp.dot(tri, mid, preferred_element_type=F32)
             + jnp.dot(tri, lo, preferred_element_type=F32)) + carry[...]
        c_ref[...] = c
        ct_ref[...] = c.T[:ct_ref.shape[0], :]
        carry[...] = c[tm - 1:tm, :]


def _logits(x, wl3, bl3, l, tm, with_cumsum):
    M, K = x.shape
    out_shape = [jax.ShapeDtypeStruct((M, LANES), F32)]
    out_specs = [pl.BlockSpec((tm, LANES), lambda i: (i, 0))]
    scratch = []
    if with_cumsum:
        out_shape += [jax.ShapeDtypeStruct((M, LANES), F32), jax.ShapeDtypeStruct((N_HEADS, M), F32)]
        out_specs += [pl.BlockSpec((tm, LANES), lambda i: (i, 0)), pl.BlockSpec((N_HEADS, tm), lambda i: (0, i))]
        scratch = [pltpu.VMEM((1, LANES), F32)]
    return pl.pallas_call(
        functools.partial(_logit_kernel, n_valid=N_HEADS, with_cumsum=with_cumsum),
        grid=(M // tm,),
        in_specs=[pl.BlockSpec((tm, K), lambda i: (i, 0)),
                  pl.BlockSpec((None, K, LANES), lambda i: (l, 0, 0)),
                  pl.BlockSpec((None, 1, LANES), lambda i: (l, 0, 0))],
        out_specs=out_specs, out_shape=out_shape, scratch_shapes=scratch,
        compiler_params=_cparams(("arbitrary",)),
        name="logits",
    )(x, wl3, bl3)


def _flash_init(m_sc, l_sc, acc_sc):
    m_sc[...] = jnp.full_like(m_sc, NEG)
    l_sc[...] = jnp.zeros_like(l_sc)
    acc_sc[...] = jnp.zeros_like(acc_sc)


def _flash_update(s, v, m_sc, l_sc, acc_sc):
    m_prev = m_sc[...]
    m_new = jnp.maximum(m_prev, jnp.max(s, axis=-1, keepdims=True))
    alpha = jnp.exp(m_prev - m_new)
    p = jnp.exp(s - m_new)
    l_sc[...] = alpha * l_sc[...] + jnp.sum(p, axis=-1, keepdims=True)
    acc_sc[...] = alpha * acc_sc[...] + jnp.dot(p.astype(BF16), v, preferred_element_type=F32)
    m_sc[...] = m_new


def _diff_attn_kernel(q_ref, k_ref, v_ref, bd_ref, bs_ref, lam_ref, g_ref, o_ref,
                      m_sc, l_sc, acc_sc, *, blk, lam0):
    qi = pl.program_id(1)
    q = q_ref[...].astype(F32)
    lane = lax.broadcasted_iota(jnp.int32, q.shape, 1)
    qq = jnp.concatenate([jnp.where(lane < DH_DIFF, q, 0.0), jnp.where(lane >= DH_DIFF, q, 0.0)], axis=0).astype(BF16)
    _flash_init(m_sc, l_sc, acc_sc)

    def step(kj, bias_ref):
        ks = pl.multiple_of(kj * blk, blk)
        k = k_ref[pl.ds(ks, blk), :]
        v = v_ref[pl.ds(ks, blk), :]
        s = lax.dot_general(qq, k, NT_DIMS, preferred_element_type=F32)
        if bias_ref is not None:
            b = bias_ref[...]
            s = s + jnp.concatenate([b, b], axis=0)
        _flash_update(s, v, m_sc, l_sc, acc_sc)

    def far_body(kj, carry):
        step(kj, None)
        return carry

    lax.fori_loop(0, jnp.maximum(qi - 1, 0), far_body, 0)

    @pl.when(qi >= 1)
    def _():
        step(qi - 1, bs_ref)

    step(qi, bd_ref)

    lam = _lam_value(lam_ref[...], lam0)
    o = acc_sc[...] / l_sc[...]
    o = o[:blk] - lam * o[blk:]
    o = o * lax.rsqrt(jnp.mean(o * o, axis=-1, keepdims=True) + LN_EPS)
    o_ref[...] = (o * g_ref[...] * (1.0 - lam0)).astype(o_ref.dtype)


def _diff_attention(zb, bias_diag, bias_sub, lam_p, g, l, lam0, blk):
    T = zb.shape[1]
    H = N_HEADS
    return pl.pallas_call(
        functools.partial(_diff_attn_kernel, blk=blk, lam0=lam0),
        grid=(H, T // blk),
        in_specs=[pl.BlockSpec((None, blk, LANES), lambda h, i: (h, i, 0)),
                  pl.BlockSpec((None, T, LANES), lambda h, i: (H + h, 0, 0)),
                  pl.BlockSpec((None, T, LANES), lambda h, i: (2 * H + h, 0, 0)),
                  pl.BlockSpec((None, blk, blk), lambda h, i: (h, 0, 0)),
                  pl.BlockSpec((None, blk, blk), lambda h, i: (h, 0, 0)),
                  pl.BlockSpec((None, 4, DH_DIFF), lambda h, i: (l, 0, 0)),
                  pl.BlockSpec((None, 1, LANES), lambda h, i: (l, 0, 0))],
        out_specs=pl.BlockSpec((None, blk, LANES), lambda h, i: (h, i, 0)),
        out_shape=jax.ShapeDtypeStruct((H, T, LANES), BF16),
        scratch_shapes=[pltpu.VMEM((2 * blk, 1), F32), pltpu.VMEM((2 * blk, 1), F32),
                        pltpu.VMEM((2 * blk, LANES), F32)],
        compiler_params=_cparams(("arbitrary", "arbitrary")),
        name="diff_attn",
    )(zb, zb, zb, bias_diag, bias_sub, lam_p, g)


def _fox_attn_kernel(q_ref, k_ref, v_ref, c_ref, ct_ref, o_ref, m_sc, l_sc, acc_sc, *, blk):
    h = pl.program_id(0)
    qi = pl.program_id(1)
    q = q_ref[...]
    c = c_ref[...]
    lane = lax.broadcasted_iota(jnp.int32, c.shape, 1)
    cq = jnp.sum(jnp.where(lane == h, c, 0.0), axis=-1, keepdims=True)
    _flash_init(m_sc, l_sc, acc_sc)

    def step(kj, masked):
        ks = pl.multiple_of(kj * blk, blk)
        k = k_ref[pl.ds(ks, blk), :]
        v = v_ref[pl.ds(ks, blk), :]
        ck = ct_ref[pl.ds(h, 1), pl.ds(ks, blk)]
        s = lax.dot_general(q, k, NT_DIMS, preferred_element_type=F32) + (cq - ck)
        if masked:
            row = lax.broadcasted_iota(jnp.int32, s.shape, 0)
            col = lax.broadcasted_iota(jnp.int32, s.shape, 1)
            s = jnp.where(row >= col, s, NEG)
        _flash_update(s, v, m_sc, l_sc, acc_sc)

    def far_body(kj, carry):
        step(kj, False)
        return carry

    lax.fori_loop(0, qi, far_body, 0)
    step(qi, True)
    o_ref[...] = (acc_sc[...] / l_sc[...]).astype(o_ref.dtype)


def _fox_attention(zb, c, ct, blk):
    T = zb.shape[1]
    H = N_HEADS
    return pl.pallas_call(
        functools.partial(_fox_attn_kernel, blk=blk),
        grid=(H, T // blk),
        in_specs=[pl.BlockSpec((None, blk, LANES), lambda h, i: (3 * H + h, i, 0)),
                  pl.BlockSpec((None, T, LANES), lambda h, i: (4 * H + h, 0, 0)),
                  pl.BlockSpec((None, T, LANES), lambda h, i: (5 * H + h, 0, 0)),
                  pl.BlockSpec((blk, LANES), lambda h, i: (i, 0)),
                  pl.BlockSpec((N_HEADS, T), lambda h, i: (0, 0))],
        out_specs=pl.BlockSpec((None, blk, LANES), lambda h, i: (h, i, 0)),
        out_shape=jax.ShapeDtypeStruct((H, T, LANES), BF16),
        scratch_shapes=[pltpu.VMEM((blk, 1), F32), pltpu.VMEM((blk, 1), F32),
                        pltpu.VMEM((blk, LANES), F32)],
        compiler_params=_cparams(("arbitrary", "arbitrary")),
        name="fox_attn",
    )(zb, zb, zb, c, ct)


def _mm_pn_kernel(*refs, n_parts, alpha):
    a_refs = refs[:n_parts]
    w_ref, r_ref, g_ref, b_ref, of_ref, ob_ref, acc = refs[n_parts:]
    k = pl.program_id(1)

    @pl.when(k == 0)
    def _():
        acc[...] = jnp.zeros_like(acc)

    def contrib(a_ref):
        a = a_ref[...]
        if a.ndim == 3:
            a = _lane_cat(a)
        acc[...] += jnp.dot(a, w_ref[...], preferred_element_type=F32)

    if n_parts == 1:
        contrib(a_refs[0])
    else:
        for p in range(n_parts):
            pl.when(k == p)(functools.partial(contrib, a_refs[p]))

    @pl.when(k == pl.num_programs(1) - 1)
    def _():
        y = _post_norm(alpha * r_ref[...] + acc[...], g_ref[...], b_ref[...])
        of_ref[...] = y
        ob_ref[...] = y.astype(BF16)


def _mm_post_norm(parts, w3, l, resid, g3, b3, ln_idx, alpha, tm, tk):
    M, D = resid.shape
    in_specs = []
    if len(parts) == 1 and parts[0].ndim == 2:
        nk = parts[0].shape[1] // tk
        in_specs.append(pl.BlockSpec((tm, tk), lambda i, k: (i, k)))
    else:
        nk = len(parts)
        for a in parts:
            if a.ndim == 3:
                in_specs.append(pl.BlockSpec((a.shape[0], tm, LANES), lambda i, k: (0, i, 0)))
            else:
                in_specs.append(pl.BlockSpec((tm, tk), lambda i, k: (i, 0)))
    in_specs += [pl.BlockSpec((None, tk, D), lambda i, k: (l, k, 0)),
                 pl.BlockSpec((tm, D), lambda i, k: (i, 0)),
                 pl.BlockSpec((None, None, 1, D), lambda i, k: (l, ln_idx, 0, 0)),
                 pl.BlockSpec((None, None, 1, D), lambda i, k: (l, ln_idx, 0, 0))]
    return pl.pallas_call(
        functools.partial(_mm_pn_kernel, n_parts=len(parts), alpha=alpha),
        grid=(M // tm, nk),
        in_specs=in_specs,
        out_specs=[pl.BlockSpec((tm, D), lambda i, k: (i, 0)), pl.BlockSpec((tm, D), lambda i, k: (i, 0))],
        out_shape=[jax.ShapeDtypeStruct((M, D), F32), jax.ShapeDtypeStruct((M, D), BF16)],
        scratch_shapes=[pltpu.VMEM((tm, D), F32)],
        compiler_params=_cparams(("arbitrary", "arbitrary")),
        name="mm_post_norm",
    )(*parts, w3, resid, g3, b3)


def _mem_attn_kernel(x_ref, wq_ref, mk_ref, mv_ref, wo_ref, r_ref, g_ref, b_ref, of_ref, ob_ref, *, alpha):
    scale = HEAD_W ** -0.5
    q = (jnp.dot(x_ref[...], wq_ref[...], preferred_element_type=F32) * scale).astype(BF16)
    mk = mk_ref[...]
    mv = mv_ref[...]
    outs = []
    for h in range(N_HEADS_MEM):
        sl = slice(h * HEAD_W, (h + 1) * HEAD_W)
        s = lax.dot_general(q[:, sl], mk[:, sl], NT_DIMS, preferred_element_type=F32)
        s = s - jnp.max(s, axis=-1, keepdims=True)
        p = jnp.exp(s)
        o = jnp.dot(p.astype(BF16), mv[:, sl], preferred_element_type=F32)
        outs.append((o / jnp.sum(p, axis=-1, keepdims=True)).astype(BF16))
    o = jnp.concatenate(outs, axis=1)
    y = jnp.dot(o, wo_ref[...], preferred_element_type=F32)
    y = _post_norm(alpha * r_ref[...] + y, g_ref[...], b_ref[...])
    of_ref[...] = y
    ob_ref[...] = y.astype(BF16)


def _mem_attention(xb, w_mq, mkb, mvb, w_mo, l, resid, g3, b3, alpha, tm):
    M, D = resid.shape
    W = w_mq.shape[2]
    NM = mkb.shape[0]
    return pl.pallas_call(
        functools.partial(_mem_attn_kernel, alpha=alpha),
        grid=(M // tm,),
        in_specs=[pl.BlockSpec((tm, D), lambda i: (i, 0)),
                  pl.BlockSpec((None, D, W), lambda i: (l, 0, 0)),
                  pl.BlockSpec((NM, W), lambda i: (0, 0)),
                  pl.BlockSpec((NM, W), lambda i: (0, 0)),
                  pl.BlockSpec((None, W, D), lambda i: (l, 0, 0)),
                  pl.BlockSpec((tm, D), lambda i: (i, 0)),
                  pl.BlockSpec((None, None, 1, D), lambda i: (l, 1, 0, 0)),
                  pl.BlockSpec((None, None, 1, D), lambda i: (l, 1, 0, 0))],
        out_specs=[pl.BlockSpec((tm, D), lambda i: (i, 0)), pl.BlockSpec((tm, D), lambda i: (i, 0))],
        out_shape=[jax.ShapeDtypeStruct((M, D), F32), jax.ShapeDtypeStruct((M, D), BF16)],
        compiler_params=_cparams(("arbitrary",)),
        name="mem_attn",
    )(xb, w_mq, mkb, mvb, w_mo, resid, g3, b3)


def _swiglu_kernel(x_ref, wg_ref, wu_ref, o_ref):
    x = x_ref[...]
    g = jnp.dot(x, wg_ref[...], preferred_element_type=F32)
    u = jnp.dot(x, wu_ref[...], preferred_element_type=F32)
    o_ref[...] = (g * jax.nn.sigmoid(g) * u).astype(o_ref.dtype)


def _swiglu_up(xb, w_ff_in, l, tm, tn):
    M, D = xb.shape
    dff = w_ff_in.shape[2] // 2
    nj = dff // tn
    return pl.pallas_call(
        _swiglu_kernel,
        grid=(nj, M // tm),
        in_specs=[pl.BlockSpec((tm, D), lambda j, i: (i, 0)),
                  pl.BlockSpec((None, D, tn), lambda j, i: (l, 0, j)),
                  pl.BlockSpec((None, D, tn), lambda j, i: (l, 0, j + nj))],
        out_specs=pl.BlockSpec((tm, tn), lambda j, i: (i, j)),
        out_shape=jax.ShapeDtypeStruct((M, dff), BF16),
        compiler_params=_cparams(("arbitrary", "arbitrary")),
        name="swiglu_up",
    )(xb, w_ff_in, w_ff_in)


def _head_mask(rows, width, row_mod):
    r = lax.broadcasted_iota(jnp.int32, (rows, width), 0)
    c = lax.broadcasted_iota(jnp.int32, (rows, width), 1)
    return jnp.right_shift(c, 7) == jnp.bitwise_and(r, row_mod - 1)


def _online_update(s, v, m_sc, l_sc, acc_sc):
    m_prev = m_sc[...]
    m_new = jnp.maximum(m_prev, jnp.max(s, axis=-1, keepdims=True))
    alpha = jnp.exp(m_prev - m_new)
    p = jnp.exp(s - m_new)
    l_sc[...] = alpha * l_sc[...] + jnp.sum(p, axis=-1, keepdims=True)
    acc_sc[...] = alpha * acc_sc[...] + jnp.dot(p.astype(BF16), v, preferred_element_type=F32)
    m_sc[...] = m_new


def _decay_kernel(pt_ref, lfn_ref, *refs, n_pages):
    P = n_pages
    lfp = refs[:P]
    dec_ref, carry = refs[P:]
    PS, H = lfp[0].shape

    @pl.when(pl.program_id(1) == 0)
    def _():
        carry[...] = lfn_ref[...][:, :H]

    tr = lax.broadcasted_iota(jnp.int32, (PS, PS), 0)
    jc = lax.broadcasted_iota(jnp.int32, (PS, PS), 1)
    later = (jc > tr).astype(BF16)
    for r in range(P - 1, -1, -1):
        x = lfp[r][...]
        hi, mid, lo = _split3(x)
        dec_ref[r] = (jnp.dot(later, hi, preferred_element_type=F32) + jnp.dot(later, mid, preferred_element_type=F32)
                      + jnp.dot(later, lo, preferred_element_type=F32)) + carry[...]
        carry[...] = carry[...] + jnp.sum(x, axis=0, keepdims=True)


def _decay(page_table, lf_new, clf, l, n_pages):
    B, per_seq = page_table.shape
    PS, H = clf.shape[2:]
    nch = per_seq // n_pages
    in_specs = [pl.BlockSpec((None, 1, LANES), lambda b, c, pt: (b, 0, 0))]
    for r in range(n_pages):
        in_specs.append(pl.BlockSpec((None, None, PS, H),
                                     lambda b, c, pt, r=r: (l, pt[b, (nch - 1 - c) * n_pages + r], 0, 0)))
    grid_spec = pltpu.PrefetchScalarGridSpec(
        num_scalar_prefetch=1, grid=(B, nch), in_specs=in_specs,
        out_specs=pl.BlockSpec((None, n_pages, PS, H), lambda b, c, pt: (b, nch - 1 - c, 0, 0)),
        scratch_shapes=[pltpu.VMEM((1, H), F32)])
    return pl.pallas_call(
        functools.partial(_decay_kernel, n_pages=n_pages),
        grid_spec=grid_spec,
        out_shape=jax.ShapeDtypeStruct((B, per_seq, PS, H), F32),
        compiler_params=_cparams(("arbitrary", "arbitrary")),
        name="decay",
    )(page_table, lf_new.reshape(B, 1, LANES), *([clf] * n_pages))


def _decode_kernel(pt_ref, qd_ref, qf_ref, kdn_ref, vdn_ref, kfn_ref, vfn_ref,
                   blast_ref, bnew_ref, lam_ref, g_ref, dec_ref, *refs, n_pages, lam0):
    P = n_pages
    kd, vd, kf, vf = (refs[i * P:(i + 1) * P] for i in range(4))
    od_ref, of_ref = refs[4 * P:4 * P + 2]
    qd_sc, qf_sc, md, ld, accd, mf, lf_sum, accf = refs[4 * P + 2:]
    c = pl.program_id(1)
    H = N_HEADS
    PS = kd[0].shape[0]
    R = PS * H

    @pl.when(c == 0)
    def _():
        q = qd_ref[...].astype(F32)
        lane = lax.broadcasted_iota(jnp.int32, q.shape, 1)
        qq = jnp.concatenate([jnp.where(lane < DH_DIFF, q, 0.0), jnp.where(lane >= DH_DIFF, q, 0.0)], axis=0)
        qd_sc[...] = qq.astype(BF16)
        qf_sc[...] = qf_ref[...]
        kn = kdn_ref[...].astype(BF16).astype(F32)
        md[...] = jnp.sum(qq * jnp.concatenate([kn, kn], axis=0), axis=-1, keepdims=True) + bnew_ref[...][:, 0:1]
        ld[...] = jnp.ones_like(ld)
        vn = vdn_ref[...].astype(BF16).astype(F32)
        accd[...] = jnp.concatenate([vn, vn], axis=0)
        mf[...] = jnp.sum(qf_ref[...].astype(F32) * kfn_ref[...].astype(BF16).astype(F32), axis=-1, keepdims=True)
        lf_sum[...] = jnp.ones_like(lf_sum)
        accf[...] = vfn_ref[...].astype(BF16).astype(F32)

    def own_head(rows):
        r = lax.broadcasted_iota(jnp.int32, (rows, R), 0)
        col = lax.broadcasted_iota(jnp.int32, (rows, R), 1)
        return jnp.bitwise_and(col, H - 1) == jnp.bitwise_and(r, H - 1)

    own_d = own_head(2 * H)
    own_f = own_head(H)
    is_last = (c == 0).astype(F32)

    for r in range(P - 1, -1, -1):
        k2 = kd[r][...].reshape(R, HEAD_W).astype(BF16)
        s = lax.dot_general(qd_sc[...], k2, NT_DIMS, preferred_element_type=F32)
        if r == P - 1:
            s = s + is_last * blast_ref[...]
        s = jnp.where(own_d, s, NEG)
        _online_update(s, vd[r][...].reshape(R, HEAD_W).astype(BF16), md, ld, accd)
        k2 = kf[r][...].reshape(R, HEAD_W).astype(BF16)
        s = lax.dot_general(qf_sc[...], k2, NT_DIMS, preferred_element_type=F32) + dec_ref[r]
        s = jnp.where(own_f, s, NEG)
        _online_update(s, vf[r][...].reshape(R, HEAD_W).astype(BF16), mf, lf_sum, accf)

    @pl.when(c == pl.num_programs(1) - 1)
    def _():
        lam = _lam_value(lam_ref[...], lam0)
        o = accd[...] / ld[...]
        o = o[:H] - lam * o[H:]
        o = o * lax.rsqrt(jnp.mean(o * o, axis=-1, keepdims=True) + LN_EPS)
        od_ref[...] = (o * g_ref[...] * (1.0 - lam0)).astype(od_ref.dtype)
        of_ref[...] = (accf[...] / lf_sum[...]).astype(of_ref.dtype)


def _decode_attention(page_table, zsb, zsf, dec, caches, bias_last, bias_new, lam_p, g, l, lam0, n_pages):
    B = zsb.shape[0]
    H = N_HEADS
    ckd, cvd, ckf, cvf = caches
    PS = ckd.shape[2]
    per_seq = page_table.shape[1]
    nch = per_seq // n_pages

    def group_spec(g_idx):
        return pl.BlockSpec((None, H, HEAD_W), lambda b, c, pt: (b, g_idx, 0))

    in_specs = [group_spec(0), group_spec(3), group_spec(1), group_spec(2), group_spec(4), group_spec(5),
                pl.BlockSpec((1, PS * H), lambda b, c, pt: (0, 0)),
                pl.BlockSpec((2 * H, LANES), lambda b, c, pt: (0, 0)),
                pl.BlockSpec((None, 4, DH_DIFF), lambda b, c, pt: (l, 0, 0)),
                pl.BlockSpec((None, 1, HEAD_W), lambda b, c, pt: (l, 0, 0)),
                pl.BlockSpec((None, n_pages, 1, PS * H), lambda b, c, pt: (b, nch - 1 - c, 0, 0))]
    args = [zsb, zsb, zsf, zsf, zsf, zsf, bias_last, bias_new, lam_p, g, dec]
    for cache in (ckd, cvd, ckf, cvf):
        for r in range(n_pages):
            in_specs.append(pl.BlockSpec((None, None, PS, H, HEAD_W),
                                         lambda b, c, pt, r=r: (l, pt[b, (nch - 1 - c) * n_pages + r], 0, 0, 0)))
            args.append(cache)
    out_spec = pl.BlockSpec((None, H, HEAD_W), lambda b, c, pt: (b, 0, 0))
    grid_spec = pltpu.PrefetchScalarGridSpec(
        num_scalar_prefetch=1, grid=(B, nch), in_specs=in_specs, out_specs=[out_spec, out_spec],
        scratch_shapes=[pltpu.VMEM((2 * H, HEAD_W), BF16), pltpu.VMEM((H, HEAD_W), BF16),
                        pltpu.VMEM((2 * H, 1), F32), pltpu.VMEM((2 * H, 1), F32), pltpu.VMEM((2 * H, HEAD_W), F32),
                        pltpu.VMEM((H, 1), F32), pltpu.VMEM((H, 1), F32), pltpu.VMEM((H, HEAD_W), F32)])
    return pl.pallas_call(
        functools.partial(_decode_kernel, n_pages=n_pages, lam0=lam0),
        grid_spec=grid_spec,
        out_shape=[jax.ShapeDtypeStruct((B, H, HEAD_W), BF16), jax.ShapeDtypeStruct((B, H, HEAD_W), BF16)],
        compiler_params=_cparams(("arbitrary", "arbitrary")),
        name="decode_attn",
    )(page_table, *args)


def _mem_decode_kernel(q_ref, mk_ref, mv_ref, o_ref):
    H8 = 8
    W = q_ref.shape[-1]
    mask = _head_mask(H8, W, H8)
    q = jnp.broadcast_to(q_ref[...].astype(F32), (H8, W))
    qbd = jnp.where(mask, q, 0.0).astype(BF16)
    s = lax.dot_general(qbd, mk_ref[...].astype(BF16), NT_DIMS, preferred_element_type=F32)
    s = s - jnp.max(s, axis=-1, keepdims=True)
    p = jnp.exp(s)
    o = jnp.dot(p.astype(BF16), mv_ref[...].astype(BF16), preferred_element_type=F32)
    o = jnp.where(mask, o / jnp.sum(p, axis=-1, keepdims=True), 0.0)
    o_ref[...] = jnp.sum(o, axis=0, keepdims=True).astype(o_ref.dtype)


def _mem_decode_attention(qb, cmk, cmv, l):
    B, W = qb.shape
    NM = cmk.shape[2]
    out = pl.pallas_call(
        _mem_decode_kernel,
        grid=(B,),
        in_specs=[pl.BlockSpec((None, 1, W), lambda b: (b, 0, 0)),
                  pl.BlockSpec((None, None, NM, W), lambda b: (l, b, 0, 0)),
                  pl.BlockSpec((None, None, NM, W), lambda b: (l, b, 0, 0))],
        out_specs=pl.BlockSpec((None, 1, W), lambda b: (b, 0, 0)),
        out_shape=jax.ShapeDtypeStruct((B, 1, W), BF16),
        compiler_params=_cparams(("arbitrary",)),
        name="mem_decode_attn",
    )(qb.reshape(B, 1, W), cmk, cmv)
    return out.reshape(B, W)


def _bias_tables(rel_bias, blk):
    assert blk >= MAX_DISTANCE
    far = rel_bias[N_BUCKETS - 1]
    rel = (rel_bias[_rel_bucket(jnp.arange(2 * blk, dtype=jnp.int32))] - far).T
    r = jnp.arange(blk)[:, None]
    c = jnp.arange(blk)[None, :]
    diag = jnp.where((r >= c)[None], rel[:, jnp.maximum(r - c, 0)], NEG)
    sub = rel[:, blk + r - c]
    return diag.astype(F32), sub.astype(F32)


def _decode_bias(rel_bias, page_size):
    assert page_size >= MAX_DISTANCE
    far = rel_bias[N_BUCKETS - 1]
    last = rel_bias[_rel_bucket(page_size - jnp.arange(page_size, dtype=jnp.int32))] - far
    new = jnp.broadcast_to((rel_bias[0] - far)[:, None], (N_HEADS, LANES))
    return last.reshape(1, page_size * N_HEADS).astype(F32), jnp.concatenate([new, new], axis=0).astype(F32)


def kernel(x_prompt, x_sample, cache_diff_k, cache_diff_v, cache_fox_k, cache_fox_v, cache_fox_logf,
           cache_mem_k, cache_mem_v, page_table, mem_prompt, w_in, b_forget, diff_lambda, diff_subln_g,
           rel_bias, w_o, w_mq, w_mkv, w_mo, w_ff_in, w_ff_out, ln_g, ln_b):
    depth = w_in.shape[0]
    alpha = (2 * depth) ** 0.25
    _, T, D = x_prompt.shape
    DB = x_sample.shape[0]
    W = N_HEADS * HEAD_W
    n_qkv = 6 * W
    pool, page_size = cache_diff_k.shape[1:3]
    n_mem = mem_prompt.shape[1]
    w_mem = w_mq.shape[2]
    blk = min(ATT_BLK, T)
    tm = min(512, T)
    assert rel_bias.shape == (N_BUCKETS, N_HEADS) and w_in.shape[2] == n_qkv + N_HEADS

    w_in_b = w_in[:, :, :n_qkv].astype(BF16)
    w_fl_b = jnp.pad(w_in[:, :, n_qkv:], ((0, 0), (0, 0), (0, LANES - N_HEADS))).astype(BF16)
    b_fl = jnp.pad(b_forget, ((0, 0), (0, LANES - N_HEADS))).reshape(depth, 1, LANES)
    w_o_b, w_mq_b, w_mkv_b, w_mo_b = (w.astype(BF16) for w in (w_o, w_mq, w_mkv, w_mo))
    w_ff_in_b, w_ff_out_b = w_ff_in.astype(BF16), w_ff_out.astype(BF16)
    ln_g4 = ln_g.reshape(depth, 3, 1, D)
    ln_b4 = ln_b.reshape(depth, 3, 1, D)
    g_sub = diff_subln_g.reshape(depth, 1, HEAD_W)
    qk_scales = (DH_DIFF ** -0.5, 1.0, 1.0, HEAD_W ** -0.5, 1.0, 1.0)
    bias_diag, bias_sub = _bias_tables(rel_bias, blk)
    bias_last, bias_new = _decode_bias(rel_bias, page_size)
    tk_ff = 512 if w_ff_out.shape[1] % 512 == 0 else w_ff_out.shape[1]

    def ffn(xf, xb, l, tmm):
        hff = _swiglu_up(xb, w_ff_in_b, l, tmm, 512 if (w_ff_in.shape[2] // 2) % 512 == 0 else w_ff_in.shape[2] // 2)
        return _mm_post_norm([hff], w_ff_out_b, l, xf, ln_g4, ln_b4, 2, alpha, tmm, tk_ff)

    xf = x_prompt.reshape(T, D)
    xb = xf.astype(BF16)
    memb = mem_prompt.reshape(n_mem, D).astype(BF16)
    p_dk, p_dv, p_fk, p_fv, p_lf, p_mk, p_mv = [], [], [], [], [], [], []
    for l in range(depth):
        lam0 = _lambda_init(l)
        zf, zb = _proj(xb, w_in_b, l, n_qkv, W, tm, scales=qk_scales, want_f32=True, b16_layout="heads")
        lf, c, ct = _logits(xb, w_fl_b, b_fl, l, tm, True)
        od = _diff_attention(zb, bias_diag, bias_sub, diff_lambda, g_sub, l, lam0, blk)
        of = _fox_attention(zb, c, ct, blk)
        xf, xb = _mm_post_norm([od, of], w_o_b, l, xf, ln_g4, ln_b4, 0, alpha, tm, W)
        mkv_f, mkv_b = _proj(memb, w_mkv_b, l, 2 * w_mem, w_mem, n_mem, want_f32=True, b16_layout="flat")
        xf, xb = _mem_attention(xb, w_mq_b, mkv_b[:, :w_mem], mkv_b[:, w_mem:], w_mo_b, l, xf, ln_g4, ln_b4, alpha, tm)
        xf, xb = ffn(xf, xb, l, tm)
        p_dk.append(zf[:, W:2 * W])
        p_dv.append(zf[:, 2 * W:3 * W])
        p_fk.append(zf[:, 4 * W:5 * W])
        p_fv.append(zf[:, 5 * W:6 * W])
        p_lf.append(lf[:, :N_HEADS])
        p_mk.append(mkv_f[:, :w_mem])
        p_mv.append(mkv_f[:, w_mem:])
    y_prompt = xf.reshape(1, T, D)

    caches = (cache_diff_k, cache_diff_v, cache_fox_k, cache_fox_v)
    cmk = cache_mem_k.reshape(depth, DB, n_mem, w_mem)
    cmv = cache_mem_v.reshape(depth, DB, n_mem, w_mem)
    per_seq = page_table.shape[1]
    n_pages = 8 if per_seq % 8 == 0 else per_seq
    xf = x_sample.reshape(DB, D)
    xb = xf.astype(BF16)
    s_dk, s_dv, s_fk, s_fv, s_lf = [], [], [], [], []
    for l in range(depth):
        lam0 = _lambda_init(l)
        zf, zb = _proj(xb, w_in_b, l, n_qkv, W, DB, scales=qk_scales, want_f32=True, b16_layout="flat")
        (lf,) = _logits(xb, w_fl_b, b_fl, l, DB, False)
        dec = _decay(page_table, lf, cache_fox_logf, l, n_pages).reshape(DB, per_seq, 1, page_size * N_HEADS)
        od, of = _decode_attention(page_table, zb.reshape(DB, 6 * N_HEADS, HEAD_W), zf.reshape(DB, 6 * N_HEADS, HEAD_W),
                                   dec, caches, bias_last, bias_new, diff_lambda, g_sub, l, lam0, n_pages)
        xf, xb = _mm_post_norm([od.reshape(DB, W), of.reshape(DB, W)], w_o_b, l, xf, ln_g4, ln_b4, 0, alpha, DB, W)
        (qm,) = _proj(xb, w_mq_b, l, w_mem, w_mem, DB, scales=(HEAD_W ** -0.5,), b16_layout="flat")
        om = _mem_decode_attention(qm, cmk, cmv, l)
        xf, xb = _mm_post_norm([om], w_mo_b, l, xf, ln_g4, ln_b4, 1, alpha, DB, w_mem)
        xf, xb = ffn(xf, xb, l, DB)
        s_dk.append(zf[:, W:2 * W])
        s_dv.append(zf[:, 2 * W:3 * W])
        s_fk.append(zf[:, 4 * W:5 * W])
        s_fv.append(zf[:, 5 * W:6 * W])
        s_lf.append(lf[:, :N_HEADS])
    y_sample = xf.reshape(DB, 1, D)

    def st(parts, *shape):
        return jnp.stack(parts).reshape(depth, *shape)

    return (y_prompt, y_sample,
            st(p_dk, 1, T, N_HEADS, HEAD_W), st(p_dv, 1, T, N_HEADS, HEAD_W),
            st(p_fk, 1, T, N_HEADS, HEAD_W), st(p_fv, 1, T, N_HEADS, HEAD_W), st(p_lf, 1, T, N_HEADS),
            st(p_mk, 1, n_mem, N_HEADS_MEM, HEAD_W), st(p_mv, 1, n_mem, N_HEADS_MEM, HEAD_W),
            st(s_dk, DB, 1, N_HEADS, HEAD_W), st(s_dv, DB, 1, N_HEADS, HEAD_W),
            st(s_fk, DB, 1, N_HEADS, HEAD_W), st(s_fv, DB, 1, N_HEADS, HEAD_W), st(s_lf, DB, 1, N_HEADS))
```

```python
import functools
import math

import jax
import jax.numpy as jnp
from jax import lax
from jax.experimental import pallas as pl
from jax.experimental.pallas import tpu as pltpu

F32 = jnp.float32
BF16 = jnp.bfloat16

LANES = 128
LOG2E = 1.4426950408889634
NEG = -1e30
LN_EPS = 1e-5
N_BUCKETS = 32
MAX_DISTANCE = 128
DH_DIFF = 64
HEAD_W = 128
N_HEADS = 8
N_HEADS_MEM = 4
ATT_BLK = 512
VMEM_LIMIT = 56 * 1024 * 1024

NT_DIMS = (((1,), (1,)), ((), ()))


def _cparams(sem):
    return pltpu.CompilerParams(dimension_semantics=sem, vmem_limit_bytes=VMEM_LIMIT)


def _lambda_init(layer):
    return 0.8 - 0.6 * math.exp(-0.3 * layer)


def _rel_bucket(dist):
    n = jnp.maximum(dist, 0)
    max_exact = N_BUCKETS // 2
    nf = jnp.maximum(n, 1).astype(F32)
    large = max_exact + (jnp.log(nf / max_exact) / math.log(MAX_DISTANCE / max_exact)
                         * (N_BUCKETS - max_exact)).astype(jnp.int32)
    large = jnp.minimum(large, N_BUCKETS - 1)
    return jnp.where(n < max_exact, n, large)


def _split3(x):
    hi = x.astype(BF16)
    r1 = x - hi.astype(F32)
    mid = r1.astype(BF16)
    lo = (r1 - mid.astype(F32)).astype(BF16)
    return hi, mid, lo


def _lane_cat(a):
    return jnp.concatenate([a[h] for h in range(a.shape[0])], axis=1)


def _post_norm(h, g, b):
    mu = jnp.mean(h, axis=-1, keepdims=True)
    d = h - mu
    var = jnp.mean(d * d, axis=-1, keepdims=True)
    return d * lax.rsqrt(var + LN_EPS) * g + b


def _lam_value(lp, lam0):
    a = jnp.sum(lp[0:1] * lp[1:2], axis=-1, keepdims=True)
    b = jnp.sum(lp[2:3] * lp[3:4], axis=-1, keepdims=True)
    return jnp.exp(a) - jnp.exp(b) + lam0


def _proj_kernel(x_ref, w_ref, *out_refs, scale, want_f32):
    acc = jnp.dot(x_ref[...], w_ref[...], preferred_element_type=F32)
    if want_f32:
        out_refs[0][...] = acc
    out_refs[-1][...] = (acc * scale).astype(BF16)


def _proj(x, w3, l, tn, tm, scale=1.0, want_f32=False):
    M, K = x.shape
    N = w3.shape[2]
    spec = pl.BlockSpec((tm, tn), lambda j, i: (i, j))
    dtypes = ([F32] if want_f32 else []) + [BF16]
    return pl.pallas_call(
        functools.partial(_proj_kernel, scale=scale, want_f32=want_f32),
        grid=(N // tn, M // tm),
        in_specs=[pl.BlockSpec((tm, K), lambda j, i: (i, 0)),
                  pl.BlockSpec((None, K, tn), lambda j, i: (l, 0, j))],
        out_specs=[spec] * len(dtypes), out_shape=[jax.ShapeDtypeStruct((M, N), d) for d in dtypes],
        compiler_params=_cparams(("arbitrary", "arbitrary")),
        name="proj",
    )(x, w3)


F32_GROUPS = (1, 2, 4, 5)


def _in_proj_kernel(x_ref, w_ref, dk_ref, dv_ref, fk_ref, fv_ref, zb_ref, *, scales):
    j = pl.program_id(1)
    acc = jnp.dot(x_ref[...], w_ref[...], preferred_element_type=F32)
    tm = acc.shape[0]
    for g, o_ref in zip(F32_GROUPS, (dk_ref, dv_ref, fk_ref, fv_ref)):
        @pl.when(j == g)
        def _(o_ref=o_ref):
            o_ref[...] = acc.reshape(tm, N_HEADS, HEAD_W)

    scale = jnp.float32(scales[-1])
    for jj in range(len(scales) - 2, -1, -1):
        scale = jnp.where(j == jj, jnp.float32(scales[jj]), scale)
    y = (acc * scale).astype(BF16)
    for hh in range(N_HEADS):
        zb_ref[hh] = y[:, hh * HEAD_W:(hh + 1) * HEAD_W]


def _in_proj(x, w3, l, tm, scales):
    M, K = x.shape
    W = N_HEADS * HEAD_W
    f32_spec = pl.BlockSpec((tm, N_HEADS, HEAD_W), lambda i, j: (i, 0, 0))
    f32_shape = jax.ShapeDtypeStruct((M, N_HEADS, HEAD_W), F32)
    return pl.pallas_call(
        functools.partial(_in_proj_kernel, scales=tuple(scales)),
        grid=(M // tm, 6),
        in_specs=[pl.BlockSpec((tm, K), lambda i, j: (i, 0)),
                  pl.BlockSpec((None, K, W), lambda i, j: (l, 0, j))],
        out_specs=[f32_spec] * 4 + [pl.BlockSpec((N_HEADS, tm, HEAD_W), lambda i, j: (j, i, 0))],
        out_shape=[f32_shape] * 4 + [jax.ShapeDtypeStruct((6 * N_HEADS, M, HEAD_W), BF16)],
        compiler_params=_cparams(("arbitrary", "arbitrary")),
        name="in_proj",
    )(x, w3)


def _logit_kernel(x_ref, w_ref, b_ref, lf_ref, *rest, n_valid, with_cumsum):
    fl = jnp.dot(x_ref[...], w_ref[...], preferred_element_type=F32) + b_ref[...]
    lf = jnp.minimum(fl, 0.0) - jnp.log1p(jnp.exp(-jnp.abs(fl)))
    tm = lf.shape[0]
    lane = lax.broadcasted_iota(jnp.int32, lf.shape, 1)
    lf = jnp.where(lane < n_valid, lf, 0.0)
    lf_ref[...] = lf
    if with_cumsum:
        ct_ref, carry = rest

        @pl.when(pl.program_id(0) == 0)
        def _():
            carry[...] = jnp.zeros_like(carry)

        row = lax.broadcasted_iota(jnp.int32, (tm, tm), 0)
        col = lax.broadcasted_iota(jnp.int32, (tm, tm), 1)
        tri = (row >= col).astype(BF16)
        hi, mid, lo = _split3(lf)
        c = (jnp.dot(tri, hi, preferred_element_type=F32)
             + jnp.dot(tri, mid, preferred_element_type=F32)
             + jnp.dot(tri, lo, preferred_element_type=F32)) + carry[...]
        ct_ref[...] = c.T[:ct_ref.shape[0], :]
        carry[...] = c[tm - 1:tm, :]


def _logits(x, wl3, bl3, l, tm, with_cumsum):
    M, K = x.shape
    out_shape = [jax.ShapeDtypeStruct((M, LANES), F32)]
    out_specs = [pl.BlockSpec((tm, LANES), lambda i: (i, 0))]
    scratch = []
    if with_cumsum:
        out_shape += [jax.ShapeDtypeStruct((N_HEADS, M), F32)]
        out_specs += [pl.BlockSpec((N_HEADS, tm), lambda i: (0, i))]
        scratch = [pltpu.VMEM((1, LANES), F32)]
    return pl.pallas_call(
        functools.partial(_logit_kernel, n_valid=N_HEADS, with_cumsum=with_cumsum),
        grid=(M // tm,),
        in_specs=[pl.BlockSpec((tm, K), lambda i: (i, 0)),
                  pl.BlockSpec((None, K, LANES), lambda i: (l, 0, 0)),
                  pl.BlockSpec((None, 1, LANES), lambda i: (l, 0, 0))],
        out_specs=out_specs, out_shape=out_shape, scratch_shapes=scratch,
        compiler_params=_cparams(("arbitrary",)),
        name="logits",
    )(x, wl3, bl3)


def _flash_init(m_sc, l_sc, acc_sc):
    m_sc[...] = jnp.full_like(m_sc, NEG)
    l_sc[...] = jnp.zeros_like(l_sc)
    acc_sc[...] = jnp.zeros_like(acc_sc)


def _flash_update(s, v, m_sc, l_sc, acc_sc):
    n = s.shape[1] // LANES
    m_prev = m_sc[...]
    m_new = jnp.maximum(m_prev, jnp.max(s, axis=-1, keepdims=True))
    alpha = jnp.exp2(m_prev - m_new)
    p = jnp.exp2(s - jnp.concatenate([m_new] * n, axis=1))
    psum = p[:, :LANES]
    for c in range(1, n):
        psum = psum + p[:, c * LANES:(c + 1) * LANES]
    l_sc[...] = alpha * l_sc[...] + psum
    acc_sc[...] = alpha * acc_sc[...] + jnp.dot(p.astype(BF16), v, preferred_element_type=F32)
    m_sc[...] = m_new


def _flash_rowsum(l_sc):
    return jnp.sum(l_sc[...], axis=-1, keepdims=True)


def _diff_attn_kernel(q_ref, k_ref, v_ref, bd_ref, bs_ref, lam_ref, g_ref, o_ref,
                      m_sc, l_sc, acc_sc, *, blk, lam0):
    qi = pl.program_id(1)
    q = q_ref[...].astype(F32)
    lane = lax.broadcasted_iota(jnp.int32, q.shape, 1)
    qq = jnp.concatenate([jnp.where(lane < DH_DIFF, q, 0.0), jnp.where(lane >= DH_DIFF, q, 0.0)], axis=0).astype(BF16)
    _flash_init(m_sc, l_sc, acc_sc)

    def step(kj, bias_ref):
        ks = pl.multiple_of(kj * blk, blk)
        k = k_ref[pl.ds(ks, blk), :]
        v = v_ref[pl.ds(ks, blk), :]
        s = lax.dot_general(qq, k, NT_DIMS, preferred_element_type=F32)
        if bias_ref is not None:
            b = bias_ref[...]
            s = s + jnp.concatenate([b, b], axis=0)
        _flash_update(s, v, m_sc, l_sc, acc_sc)

    def far_body(kj, carry):
        step(kj, None)
        return carry

    lax.fori_loop(0, jnp.maximum(qi - 1, 0), far_body, 0)

    @pl.when(qi >= 1)
    def _():
        step(qi - 1, bs_ref)

    step(qi, bd_ref)

    lam = _lam_value(lam_ref[...], lam0)
    o = acc_sc[...] / _flash_rowsum(l_sc)
    o = o[:blk] - lam * o[blk:]
    o = o * lax.rsqrt(jnp.mean(o * o, axis=-1, keepdims=True) + LN_EPS)
    o_ref[...] = (o * g_ref[...] * (1.0 - lam0)).astype(o_ref.dtype)


def _diff_attention(zb, bias_diag, bias_sub, lam_p, g, l, lam0, blk):
    T = zb.shape[1]
    H = N_HEADS
    return pl.pallas_call(
        functools.partial(_diff_attn_kernel, blk=blk, lam0=lam0),
        grid=(H, T // blk),
        in_specs=[pl.BlockSpec((None, blk, LANES), lambda h, i: (h, i, 0)),
                  pl.BlockSpec((None, T, LANES), lambda h, i: (H + h, 0, 0)),
                  pl.BlockSpec((None, T, LANES), lambda h, i: (2 * H + h, 0, 0)),
                  pl.BlockSpec((None, blk, blk), lambda h, i: (h, 0, 0)),
                  pl.BlockSpec((None, blk, blk), lambda h, i: (h, 0, 0)),
                  pl.BlockSpec((None, 4, DH_DIFF), lambda h, i: (l, 0, 0)),
                  pl.BlockSpec((None, 1, LANES), lambda h, i: (l, 0, 0))],
        out_specs=pl.BlockSpec((None, blk, LANES), lambda h, i: (h, i, 0)),
        out_shape=jax.ShapeDtypeStruct((H, T, LANES), BF16),
        scratch_shapes=[pltpu.VMEM((2 * blk, LANES), F32), pltpu.VMEM((2 * blk, LANES), F32),
                        pltpu.VMEM((2 * blk, LANES), F32)],
        compiler_params=_cparams(("arbitrary", "arbitrary")),
        name="diff_attn",
    )(zb, zb, zb, bias_diag, bias_sub, lam_p, g)


def _fox_attn_kernel(q_ref, k_ref, v_ref, ct_ref, o_ref, m_sc, l_sc, acc_sc, *, blk):
    h = pl.program_id(0)
    qi = pl.program_id(1)
    q = q_ref[...]
    qs = pl.multiple_of(qi * blk, blk)
    c_ref0 = ct_ref[pl.ds(h, 1), pl.ds(qs, blk)][:, 0:1]
    _flash_init(m_sc, l_sc, acc_sc)

    def step(kj, masked):
        ks = pl.multiple_of(kj * blk, blk)
        k = k_ref[pl.ds(ks, blk), :]
        v = v_ref[pl.ds(ks, blk), :]
        ck = (ct_ref[pl.ds(h, 1), pl.ds(ks, blk)] - c_ref0) * LOG2E
        s = lax.dot_general(q, k, NT_DIMS, preferred_element_type=F32) - ck
        if masked:
            row = lax.broadcasted_iota(jnp.int32, s.shape, 0)
            col = lax.broadcasted_iota(jnp.int32, s.shape, 1)
            s = jnp.where(row >= col, s, NEG)
        _flash_update(s, v, m_sc, l_sc, acc_sc)

    def far_body(kj, carry):
        step(kj, False)
        return carry

    lax.fori_loop(0, qi, far_body, 0)
    step(qi, True)
    o_ref[...] = (acc_sc[...] / _flash_rowsum(l_sc)).astype(o_ref.dtype)


def _fox_attention(zb, ct, blk):
    T = zb.shape[1]
    H = N_HEADS
    return pl.pallas_call(
        functools.partial(_fox_attn_kernel, blk=blk),
        grid=(H, T // blk),
        in_specs=[pl.BlockSpec((None, blk, LANES), lambda h, i: (3 * H + h, i, 0)),
                  pl.BlockSpec((None, T, LANES), lambda h, i: (4 * H + h, 0, 0)),
                  pl.BlockSpec((None, T, LANES), lambda h, i: (5 * H + h, 0, 0)),
                  pl.BlockSpec((N_HEADS, T), lambda h, i: (0, 0))],
        out_specs=pl.BlockSpec((None, blk, LANES), lambda h, i: (h, i, 0)),
        out_shape=jax.ShapeDtypeStruct((H, T, LANES), BF16),
        scratch_shapes=[pltpu.VMEM((blk, LANES), F32), pltpu.VMEM((blk, LANES), F32),
                        pltpu.VMEM((blk, LANES), F32)],
        compiler_params=_cparams(("arbitrary", "arbitrary")),
        name="fox_attn",
    )(zb, zb, zb, ct)


def _mm_pn_kernel(*refs, n_parts, alpha, tiled):
    a_refs = refs[:n_parts]
    w_ref, r_ref, g_ref, b_ref, of_ref, ob_ref = refs[n_parts:n_parts + 6]

    def finish(y):
        y = _post_norm(alpha * r_ref[...] + y, g_ref[...], b_ref[...])
        of_ref[...] = y
        ob_ref[...] = y.astype(BF16)

    if not tiled:
        y = None
        row = 0
        for a_ref in a_refs:
            a = a_ref[...]
            if a.ndim == 3:
                a = _lane_cat(a)
            d = jnp.dot(a, w_ref[row:row + a.shape[1], :], preferred_element_type=F32)
            y = d if y is None else y + d
            row += a.shape[1]
        finish(y)
        return

    acc = refs[n_parts + 6]
    k = pl.program_id(1)

    @pl.when(k == 0)
    def _():
        acc[...] = jnp.zeros_like(acc)

    acc[...] += jnp.dot(a_refs[0][...], w_ref[...], preferred_element_type=F32)

    @pl.when(k == pl.num_programs(1) - 1)
    def _():
        finish(acc[...])


def _mm_post_norm(parts, w3, l, resid, g3, b3, ln_idx, alpha, tm, tk=None):
    M, D = resid.shape
    tiled = tk is not None
    in_specs = []
    if tiled:
        assert len(parts) == 1 and parts[0].ndim == 2 and parts[0].shape[1] % tk == 0
        nk = parts[0].shape[1] // tk
        in_specs.append(pl.BlockSpec((tm, tk), lambda i, k: (i, k)))
        in_specs.append(pl.BlockSpec((None, tk, D), lambda i, k: (l, k, 0)))
    else:
        nk = 1
        for a in parts:
            if a.ndim == 3:
                in_specs.append(pl.BlockSpec((a.shape[0], tm, LANES), lambda i, k: (0, i, 0)))
            else:
                in_specs.append(pl.BlockSpec((tm, a.shape[1]), lambda i, k: (i, 0)))
        in_specs.append(pl.BlockSpec((None, w3.shape[1], D), lambda i, k: (l, 0, 0)))
    in_specs += [pl.BlockSpec((tm, D), lambda i, k: (i, 0)),
                 pl.BlockSpec((None, None, 1, D), lambda i, k: (l, ln_idx, 0, 0)),
                 pl.BlockSpec((None, None, 1, D), lambda i, k: (l, ln_idx, 0, 0))]
    return pl.pallas_call(
        functools.partial(_mm_pn_kernel, n_parts=len(parts), alpha=alpha, tiled=tiled),
        grid=(M // tm, nk),
        in_specs=in_specs,
        out_specs=[pl.BlockSpec((tm, D), lambda i, k: (i, 0)), pl.BlockSpec((tm, D), lambda i, k: (i, 0))],
        out_shape=[jax.ShapeDtypeStruct((M, D), F32), jax.ShapeDtypeStruct((M, D), BF16)],
        scratch_shapes=[pltpu.VMEM((tm, D), F32)] if tiled else [],
        compiler_params=_cparams(("arbitrary", "arbitrary")),
        name="mm_post_norm",
    )(*parts, w3, resid, g3, b3)


def _mem_attn_kernel(x_ref, wq_ref, mk_ref, mv_ref, wo_ref, r_ref, g_ref, b_ref, of_ref, ob_ref, *, alpha):
    scale = HEAD_W ** -0.5
    q = (jnp.dot(x_ref[...], wq_ref[...], preferred_element_type=F32) * scale).astype(BF16)
    mk = mk_ref[...]
    mv = mv_ref[...]
    outs = []
    for h in range(N_HEADS_MEM):
        sl = slice(h * HEAD_W, (h + 1) * HEAD_W)
        s = lax.dot_general(q[:, sl], mk[:, sl], NT_DIMS, preferred_element_type=F32)
        s = s - jnp.max(s, axis=-1, keepdims=True)
        p = jnp.exp(s)
        o = jnp.dot(p.astype(BF16), mv[:, sl], preferred_element_type=F32)
        outs.append((o / jnp.sum(p, axis=-1, keepdims=True)).astype(BF16))
    o = jnp.concatenate(outs, axis=1)
    y = jnp.dot(o, wo_ref[...], preferred_element_type=F32)
    y = _post_norm(alpha * r_ref[...] + y, g_ref[...], b_ref[...])
    of_ref[...] = y
    ob_ref[...] = y.astype(BF16)


def _mem_attention(xb, w_mq, mkb, mvb, w_mo, l, resid, g3, b3, alpha, tm):
    M, D = resid.shape
    W = w_mq.shape[2]
    NM = mkb.shape[0]
    return pl.pallas_call(
        functools.partial(_mem_attn_kernel, alpha=alpha),
        grid=(M // tm,),
        in_specs=[pl.BlockSpec((tm, D), lambda i: (i, 0)),
                  pl.BlockSpec((None, D, W), lambda i: (l, 0, 0)),
                  pl.BlockSpec((NM, W), lambda i: (0, 0)),
                  pl.BlockSpec((NM, W), lambda i: (0, 0)),
                  pl.BlockSpec((None, W, D), lambda i: (l, 0, 0)),
                  pl.BlockSpec((tm, D), lambda i: (i, 0)),
                  pl.BlockSpec((None, None, 1, D), lambda i: (l, 1, 0, 0)),
                  pl.BlockSpec((None, None, 1, D), lambda i: (l, 1, 0, 0))],
        out_specs=[pl.BlockSpec((tm, D), lambda i: (i, 0)), pl.BlockSpec((tm, D), lambda i: (i, 0))],
        out_shape=[jax.ShapeDtypeStruct((M, D), F32), jax.ShapeDtypeStruct((M, D), BF16)],
        compiler_params=_cparams(("arbitrary",)),
        name="mem_attn",
    )(xb, w_mq, mkb, mvb, w_mo, resid, g3, b3)


def _swiglu_kernel(x_ref, wg_ref, wu_ref, o_ref):
    x = x_ref[...]
    g = jnp.dot(x, wg_ref[...], preferred_element_type=F32)
    u = jnp.dot(x, wu_ref[...], preferred_element_type=F32)
    o_ref[...] = (g * jax.nn.sigmoid(g) * u).astype(o_ref.dtype)


def _swiglu_up(xb, w_ff_in, l, tm, tn):
    M, D = xb.shape
    dff = w_ff_in.shape[2] // 2
    nj = dff // tn
    return pl.pallas_call(
        _swiglu_kernel,
        grid=(nj, M // tm),
        in_specs=[pl.BlockSpec((tm, D), lambda j, i: (i, 0)),
                  pl.BlockSpec((None, D, tn), lambda j, i: (l, 0, j)),
                  pl.BlockSpec((None, D, tn), lambda j, i: (l, 0, j + nj))],
        out_specs=pl.BlockSpec((tm, tn), lambda j, i: (i, j)),
        out_shape=jax.ShapeDtypeStruct((M, dff), BF16),
        compiler_params=_cparams(("arbitrary", "arbitrary")),
        name="swiglu_up",
    )(xb, w_ff_in, w_ff_in)


def _head_mask(rows, width, row_mod):
    r = lax.broadcasted_iota(jnp.int32, (rows, width), 0)
    c = lax.broadcasted_iota(jnp.int32, (rows, width), 1)
    return jnp.right_shift(c, 7) == jnp.bitwise_and(r, row_mod - 1)


def _lane_suffix_tile(x):
    row = lax.broadcasted_iota(jnp.int32, x.shape, 0)
    lane = lax.broadcasted_iota(jnp.int32, x.shape, 1)
    y = x
    z = x
    for sh in (8, 16, 32, 64):
        shifted = pltpu.roll(y, LANES - sh, 1)
        y = y + jnp.where(lane + sh < LANES, shifted, 0.0)
        z = z + pltpu.roll(z, sh, 1)
    a = z
    for sh in (1, 2, 4):
        shifted = pltpu.roll(a, 8 - sh, 0)
        a = a + jnp.where(row + sh < 8, shifted, 0.0)
    return (y - x) + (a - z), a[0:1, :]


def _decode_kernel(pt_ref, qd_ref, qf_ref, kdn_ref, vdn_ref, kfn_ref, vfn_ref, lfn_ref,
                   blast_ref, bnew_ref, lam_ref, g_ref, *refs, n_pages, lam0):
    P = n_pages
    kd, vd, kf, vf, lfp = (refs[i * P:(i + 1) * P] for i in range(5))
    od_ref, of_ref = refs[5 * P:5 * P + 2]
    qd_sc, qf_sc, md, ld, accd, mf, lf_sum, accf, carry = refs[5 * P + 2:]
    c = pl.program_id(1)
    H = N_HEADS
    PS = kd[0].shape[0]
    R = PS * H

    @pl.when(c == 0)
    def _():
        q = qd_ref[...].astype(F32)
        lane = lax.broadcasted_iota(jnp.int32, q.shape, 1)
        qq = jnp.concatenate([jnp.where(lane < DH_DIFF, q, 0.0), jnp.where(lane >= DH_DIFF, q, 0.0)], axis=0)
        qd_sc[...] = qq.astype(BF16)
        qf_sc[...] = qf_ref[...]
        kn = kdn_ref[...].astype(BF16).astype(F32)
        m0 = jnp.sum(qq * jnp.concatenate([kn, kn], axis=0), axis=-1, keepdims=True) + bnew_ref[...][:, 0:1]
        md[...] = jnp.broadcast_to(m0, md.shape)
        ld[...] = jnp.full_like(ld, 1.0 / LANES)
        vn = vdn_ref[...].astype(BF16).astype(F32)
        accd[...] = jnp.concatenate([vn, vn], axis=0)
        m0 = jnp.sum(qf_ref[...].astype(F32) * kfn_ref[...].astype(BF16).astype(F32), axis=-1, keepdims=True)
        mf[...] = jnp.broadcast_to(m0, mf.shape)
        lf_sum[...] = jnp.full_like(lf_sum, 1.0 / LANES)
        accf[...] = vfn_ref[...].astype(BF16).astype(F32)
        lane1 = lax.broadcasted_iota(jnp.int32, (1, LANES), 1)
        v = jnp.where(lane1 < H, lfn_ref[...], 0.0)
        for sh in (8, 16, 32, 64):
            v = v + pltpu.roll(v, sh, 1)
        carry[...] = v

    def own_head(rows):
        r = lax.broadcasted_iota(jnp.int32, (rows, R), 0)
        col = lax.broadcasted_iota(jnp.int32, (rows, R), 1)
        return jnp.bitwise_and(col, H - 1) == jnp.bitwise_and(r, H - 1)

    own_d = own_head(2 * H)
    own_f = own_head(H)
    is_last = (c == 0).astype(F32)

    dec = [None] * P
    cv = carry[...]
    for r in range(P - 1, -1, -1):
        inner, tot = _lane_suffix_tile(lfp[r][...])
        tile = (inner + cv) * LOG2E
        dec[r] = jnp.concatenate([tile[i:i + 1, :] for i in range(8)], axis=1)
        cv = cv + tot
    carry[...] = cv

    sd, sf = [], []
    for r in range(P):
        s = lax.dot_general(qd_sc[...], kd[r][...].reshape(R, HEAD_W).astype(BF16), NT_DIMS,
                            preferred_element_type=F32)
        if r == P - 1:
            s = s + is_last * blast_ref[...]
        sd.append(jnp.where(own_d, s, NEG))
        s = lax.dot_general(qf_sc[...], kf[r][...].reshape(R, HEAD_W).astype(BF16), NT_DIMS,
                            preferred_element_type=F32) + dec[r]
        sf.append(jnp.where(own_f, s, NEG))

    def update(s_pages, v_refs, m_sc, l_sc, acc_sc):
        m_prev = m_sc[...]
        m_cur = s_pages[0]
        for s in s_pages[1:]:
            m_cur = jnp.maximum(m_cur, s)
        m_new = jnp.maximum(m_prev, jnp.max(m_cur, axis=-1, keepdims=True))
        alpha = jnp.exp2(m_prev - m_new)
        m_sc[...] = m_new
        m_wide = jnp.concatenate([m_new] * (R // LANES), axis=1)
        acc = alpha * acc_sc[...]
        lsum = alpha * l_sc[...]
        for s, v_ref in zip(s_pages, v_refs):
            p = jnp.exp2(s - m_wide)
            for i in range(R // LANES):
                lsum = lsum + p[:, i * LANES:(i + 1) * LANES]
            acc = acc + jnp.dot(p.astype(BF16), v_ref[...].reshape(R, HEAD_W).astype(BF16),
                                preferred_element_type=F32)
        l_sc[...] = lsum
        acc_sc[...] = acc

    update(sd, vd, md, ld, accd)
    update(sf, vf, mf, lf_sum, accf)

    @pl.when(c == pl.num_programs(1) - 1)
    def _():
        lam = _lam_value(lam_ref[...], lam0)
        o = accd[...] / jnp.sum(ld[...], axis=-1, keepdims=True)
        o = o[:H] - lam * o[H:]
        o = o * lax.rsqrt(jnp.mean(o * o, axis=-1, keepdims=True) + LN_EPS)
        od_ref[...] = (o * g_ref[...] * (1.0 - lam0)).astype(od_ref.dtype)
        of_ref[...] = (accf[...] / jnp.sum(lf_sum[...], axis=-1, keepdims=True)).astype(of_ref.dtype)


def _decode_attention(page_table, zsb, new_kv, lf_new, caches, bias_last, bias_new, lam_p, g, l, lam0, n_pages):
    B = zsb.shape[0]
    H = N_HEADS
    ckd, cvd, ckf, cvf, clf = caches
    PS = ckd.shape[2]
    per_seq = page_table.shape[1]
    nch = per_seq // n_pages

    def group_spec(g_idx):
        return pl.BlockSpec((None, H, HEAD_W), lambda b, c, pt: (b, g_idx, 0))

    new_spec = pl.BlockSpec((None, H, HEAD_W), lambda b, c, pt: (b, 0, 0))
    in_specs = [group_spec(0), group_spec(3), new_spec, new_spec, new_spec, new_spec,
                pl.BlockSpec((None, 1, LANES), lambda b, c, pt: (b, 0, 0)),
                pl.BlockSpec((1, PS * H), lambda b, c, pt: (0, 0)),
                pl.BlockSpec((2 * H, LANES), lambda b, c, pt: (0, 0)),
                pl.BlockSpec((None, 4, DH_DIFF), lambda b, c, pt: (l, 0, 0)),
                pl.BlockSpec((None, 1, HEAD_W), lambda b, c, pt: (l, 0, 0))]
    args = [zsb, zsb, *new_kv, lf_new.reshape(B, 1, LANES), bias_last, bias_new, lam_p, g]
    for cache in (ckd, cvd, ckf, cvf):
        for r in range(n_pages):
            in_specs.append(pl.BlockSpec((None, None, PS, H, HEAD_W),
                                         lambda b, c, pt, r=r: (l, pt[b, (nch - 1 - c) * n_pages + r], 0, 0, 0)))
            args.append(cache)
    for r in range(n_pages):
        in_specs.append(pl.BlockSpec((None, None, 8, LANES),
                                     lambda b, c, pt, r=r: (l, pt[b, (nch - 1 - c) * n_pages + r], 0, 0)))
        args.append(clf)
    out_spec = pl.BlockSpec((None, H, HEAD_W), lambda b, c, pt: (b, 0, 0))
    grid_spec = pltpu.PrefetchScalarGridSpec(
        num_scalar_prefetch=1, grid=(B, nch), in_specs=in_specs, out_specs=[out_spec, out_spec],
        scratch_shapes=[pltpu.VMEM((2 * H, HEAD_W), BF16), pltpu.VMEM((H, HEAD_W), BF16),
                        pltpu.VMEM((2 * H, LANES), F32), pltpu.VMEM((2 * H, LANES), F32),
                        pltpu.VMEM((2 * H, HEAD_W), F32),
                        pltpu.VMEM((H, LANES), F32), pltpu.VMEM((H, LANES), F32), pltpu.VMEM((H, HEAD_W), F32),
                        pltpu.VMEM((1, LANES), F32)])
    return pl.pallas_call(
        functools.partial(_decode_kernel, n_pages=n_pages, lam0=lam0),
        grid_spec=grid_spec,
        out_shape=[jax.ShapeDtypeStruct((B, H, HEAD_W), BF16), jax.ShapeDtypeStruct((B, H, HEAD_W), BF16)],
        compiler_params=_cparams(("arbitrary", "arbitrary")),
        name="decode_attn",
    )(page_table, *args)


def _mem_decode_kernel(q_ref, mk_ref, mv_ref, o_ref):
    H8 = 8
    W = q_ref.shape[-1]
    mask = _head_mask(H8, W, H8)
    q = jnp.broadcast_to(q_ref[...].astype(F32), (H8, W))
    qbd = jnp.where(mask, q, 0.0).astype(BF16)
    s = lax.dot_general(qbd, mk_ref[...].astype(BF16), NT_DIMS, preferred_element_type=F32)
    s = s - jnp.max(s, axis=-1, keepdims=True)
    p = jnp.exp(s)
    o = jnp.dot(p.astype(BF16), mv_ref[...].astype(BF16), preferred_element_type=F32)
    o = jnp.where(mask, o / jnp.sum(p, axis=-1, keepdims=True), 0.0)
    o_ref[...] = jnp.sum(o, axis=0, keepdims=True).astype(o_ref.dtype)


def _mem_decode_attention(qb, cmk, cmv, l):
    B, W = qb.shape
    NM = cmk.shape[2]
    out = pl.pallas_call(
        _mem_decode_kernel,
        grid=(B,),
        in_specs=[pl.BlockSpec((None, 1, W), lambda b: (b, 0, 0)),
                  pl.BlockSpec((None, None, NM, W), lambda b: (l, b, 0, 0)),
                  pl.BlockSpec((None, None, NM, W), lambda b: (l, b, 0, 0))],
        out_specs=pl.BlockSpec((None, 1, W), lambda b: (b, 0, 0)),
        out_shape=jax.ShapeDtypeStruct((B, 1, W), BF16),
        compiler_params=_cparams(("arbitrary",)),
        name="mem_decode_attn",
    )(qb.reshape(B, 1, W), cmk, cmv)
    return out.reshape(B, W)


def _bias_by_distance(rel_bias, dist):
    bucket = _rel_bucket(dist)
    rel = (rel_bias - rel_bias[N_BUCKETS - 1]) * LOG2E
    out = jnp.zeros((rel_bias.shape[1],) + dist.shape, F32)
    for b in range(N_BUCKETS - 1):
        out = jnp.where((bucket == b)[None], rel[b].reshape((-1,) + (1,) * dist.ndim), out)
    return out


def _bias_tables(rel_bias, blk):
    assert blk >= MAX_DISTANCE
    r = jnp.arange(blk, dtype=jnp.int32)[:, None]
    c = jnp.arange(blk, dtype=jnp.int32)[None, :]
    diag = jnp.where((r >= c)[None], _bias_by_distance(rel_bias, jnp.maximum(r - c, 0)), NEG)
    sub = _bias_by_distance(rel_bias, blk + r - c)
    return diag, sub


def _decode_bias(rel_bias, page_size):
    assert page_size >= MAX_DISTANCE
    last = _bias_by_distance(rel_bias, page_size - jnp.arange(page_size, dtype=jnp.int32))
    new = jnp.broadcast_to(_bias_by_distance(rel_bias, jnp.zeros((1,), jnp.int32)), (N_HEADS, LANES))
    return last.T.reshape(1, page_size * N_HEADS), jnp.concatenate([new, new], axis=0)


def kernel(x_prompt, x_sample, cache_diff_k, cache_diff_v, cache_fox_k, cache_fox_v, cache_fox_logf,
           cache_mem_k, cache_mem_v, page_table, mem_prompt, w_in, b_forget, diff_lambda, diff_subln_g,
           rel_bias, w_o, w_mq, w_mkv, w_mo, w_ff_in, w_ff_out, ln_g, ln_b):
    depth = w_in.shape[0]
    alpha = (2 * depth) ** 0.25
    _, T, D = x_prompt.shape
    DB = x_sample.shape[0]
    W = N_HEADS * HEAD_W
    n_qkv = 6 * W
    pool, page_size = cache_diff_k.shape[1:3]
    n_mem = mem_prompt.shape[1]
    w_mem = w_mq.shape[2]
    blk = min(ATT_BLK, T)
    blk_fox = 2 * blk if T % (2 * blk) == 0 else blk
    tm = min(512, T)
    assert rel_bias.shape == (N_BUCKETS, N_HEADS) and w_in.shape[2] == n_qkv + N_HEADS

    w_in_b = w_in[:, :, :n_qkv].astype(BF16)
    w_fl_b = jnp.pad(w_in[:, :, n_qkv:], ((0, 0), (0, 0), (0, LANES - N_HEADS))).astype(BF16)
    b_fl = jnp.pad(b_forget, ((0, 0), (0, LANES - N_HEADS))).reshape(depth, 1, LANES)
    w_o_b, w_mq_b, w_mkv_b, w_mo_b = (w.astype(BF16) for w in (w_o, w_mq, w_mkv, w_mo))
    w_ff_in_b, w_ff_out_b = w_ff_in.astype(BF16), w_ff_out.astype(BF16)
    ln_g4 = ln_g.reshape(depth, 3, 1, D)
    ln_b4 = ln_b.reshape(depth, 3, 1, D)
    g_sub = diff_subln_g.reshape(depth, 1, HEAD_W)
    qk_scales = (DH_DIFF ** -0.5 * LOG2E, 1.0, 1.0, HEAD_W ** -0.5 * LOG2E, 1.0, 1.0)
    bias_diag, bias_sub = _bias_tables(rel_bias, blk)
    bias_last, bias_new = _decode_bias(rel_bias, page_size)
    d_ff = w_ff_out.shape[1]
    tn_ff = 512 if d_ff % 512 == 0 else d_ff
    tk_ff = d_ff // 4 if d_ff % (4 * LANES) == 0 else d_ff

    def ffn(xf, xb, l, tmm):
        hff = _swiglu_up(xb, w_ff_in_b, l, tmm, tn_ff)
        return _mm_post_norm([hff], w_ff_out_b, l, xf, ln_g4, ln_b4, 2, alpha, tmm, tk_ff)

    xf = x_prompt.reshape(T, D)
    xb = xf.astype(BF16)
    memb = mem_prompt.reshape(n_mem, D).astype(BF16)
    p_dk, p_dv, p_fk, p_fv, p_lf, p_mk, p_mv = [], [], [], [], [], [], []
    for l in range(depth):
        lam0 = _lambda_init(l)
        dk, dv, fk, fv, zb = _in_proj(xb, w_in_b, l, tm, qk_scales)
        lf, ct = _logits(xb, w_fl_b, b_fl, l, tm, True)
        od = _diff_attention(zb, bias_diag, bias_sub, diff_lambda, g_sub, l, lam0, blk)
        of = _fox_attention(zb, ct, blk_fox)
        xf, xb = _mm_post_norm([od, of], w_o_b, l, xf, ln_g4, ln_b4, 0, alpha, tm)
        mkv_f, mkv_b = _proj(memb, w_mkv_b, l, w_mem, n_mem, want_f32=True)
        xf, xb = _mem_attention(xb, w_mq_b, mkv_b[:, :w_mem], mkv_b[:, w_mem:], w_mo_b, l, xf, ln_g4, ln_b4, alpha, tm)
        xf, xb = ffn(xf, xb, l, tm)
        p_dk.append(dk)
        p_dv.append(dv)
        p_fk.append(fk)
        p_fv.append(fv)
        p_lf.append(lf[:, :N_HEADS])
        p_mk.append(mkv_f[:, :w_mem])
        p_mv.append(mkv_f[:, w_mem:])
    y_prompt = xf.reshape(1, T, D)

    assert page_size * N_HEADS == 8 * LANES
    caches = (cache_diff_k, cache_diff_v, cache_fox_k, cache_fox_v, cache_fox_logf.reshape(depth, pool, 8, LANES))
    cmk = cache_mem_k.reshape(depth, DB, n_mem, w_mem)
    cmv = cache_mem_v.reshape(depth, DB, n_mem, w_mem)
    per_seq = page_table.shape[1]
    n_pages = 8 if per_seq % 8 == 0 else per_seq
    xf = x_sample.reshape(DB, D)
    xb = xf.astype(BF16)
    s_dk, s_dv, s_fk, s_fv, s_lf = [], [], [], [], []
    for l in range(depth):
        lam0 = _lambda_init(l)
        dk, dv, fk, fv, zb = _in_proj(xb, w_in_b, l, DB, qk_scales)
        (lf,) = _logits(xb, w_fl_b, b_fl, l, DB, False)
        od, of = _decode_attention(page_table, jnp.swapaxes(zb, 0, 1), (dk, dv, fk, fv), lf, caches,
                                   bias_last, bias_new, diff_lambda, g_sub, l, lam0, n_pages)
        xf, xb = _mm_post_norm([od.reshape(DB, W), of.reshape(DB, W)], w_o_b, l, xf, ln_g4, ln_b4, 0, alpha, DB)
        (qm,) = _proj(xb, w_mq_b, l, w_mem, DB, scale=HEAD_W ** -0.5)
        om = _mem_decode_attention(qm, cmk, cmv, l)
        xf, xb = _mm_post_norm([om], w_mo_b, l, xf, ln_g4, ln_b4, 1, alpha, DB)
        xf, xb = ffn(xf, xb, l, DB)
        s_dk.append(dk)
        s_dv.append(dv)
        s_fk.append(fk)
        s_fv.append(fv)
        s_lf.append(lf[:, :N_HEADS])
    y_sample = xf.reshape(DB, 1, D)

    def st(parts, *shape):
        return jnp.stack(parts).reshape(depth, *shape)

    return (y_prompt, y_sample,
            st(p_dk, 1, T, N_HEADS, HEAD_W), st(p_dv, 1, T, N_HEADS, HEAD_W),
            st(p_fk, 1, T, N_HEADS, HEAD_W), st(p_fv, 1, T, N_HEADS, HEAD_W), st(p_lf, 1, T, N_HEADS),
            st(p_mk, 1, n_mem, N_HEADS_MEM, HEAD_W), st(p_mv, 1, n_mem, N_HEADS_MEM, HEAD_W),
            st(s_dk, DB, 1, N_HEADS, HEAD_W), st(s_dv, DB, 1, N_HEADS, HEAD_W),
            st(s_fk, DB, 1, N_HEADS, HEAD_W), st(s_fv, DB, 1, N_HEADS, HEAD_W), st(s_lf, DB, 1, N_HEADS))
```

```python
import functools
import math

import jax
import jax.numpy as jnp
from jax import lax
from jax.experimental import pallas as pl
from jax.experimental.pallas import tpu as pltpu

F32 = jnp.float32
BF16 = jnp.bfloat16

LANES = 128
LOG2E = 1.4426950408889634
NEG = -1e30
LN_EPS = 1e-5
N_BUCKETS = 32
MAX_DISTANCE = 128
DH_DIFF = 64
HEAD_W = 128
N_HEADS = 8
N_HEADS_MEM = 4
ATT_BLK = 512
VMEM_LIMIT = 56 * 1024 * 1024

NT_DIMS = (((1,), (1,)), ((), ()))


def _cparams(sem):
    return pltpu.CompilerParams(dimension_semantics=sem, vmem_limit_bytes=VMEM_LIMIT)


def _lambda_init(layer):
    return 0.8 - 0.6 * math.exp(-0.3 * layer)


def _rel_bucket(dist):
    n = jnp.maximum(dist, 0)
    max_exact = N_BUCKETS // 2
    nf = jnp.maximum(n, 1).astype(F32)
    large = max_exact + (jnp.log(nf / max_exact) / math.log(MAX_DISTANCE / max_exact)
                         * (N_BUCKETS - max_exact)).astype(jnp.int32)
    large = jnp.minimum(large, N_BUCKETS - 1)
    return jnp.where(n < max_exact, n, large)


def _split3(x):
    hi = x.astype(BF16)
    r1 = x - hi.astype(F32)
    mid = r1.astype(BF16)
    lo = (r1 - mid.astype(F32)).astype(BF16)
    return hi, mid, lo


def _lane_cat(a):
    return jnp.concatenate([a[h] for h in range(a.shape[0])], axis=1)


def _post_norm(h, g, b):
    mu = jnp.mean(h, axis=-1, keepdims=True)
    d = h - mu
    var = jnp.mean(d * d, axis=-1, keepdims=True)
    return d * lax.rsqrt(var + LN_EPS) * g + b


def _lam_value(lp, lam0):
    a = jnp.sum(lp[0:1] * lp[1:2], axis=-1, keepdims=True)
    b = jnp.sum(lp[2:3] * lp[3:4], axis=-1, keepdims=True)
    return jnp.exp(a) - jnp.exp(b) + lam0


def _proj_kernel(x_ref, w_ref, *out_refs, scale, want_f32):
    acc = jnp.dot(x_ref[...], w_ref[...], preferred_element_type=F32)
    if want_f32:
        out_refs[0][...] = acc
    out_refs[-1][...] = (acc * scale).astype(BF16)


def _proj(x, w3, l, tn, tm, scale=1.0, want_f32=False):
    M, K = x.shape
    N = w3.shape[2]
    spec = pl.BlockSpec((tm, tn), lambda j, i: (i, j))
    dtypes = ([F32] if want_f32 else []) + [BF16]
    return pl.pallas_call(
        functools.partial(_proj_kernel, scale=scale, want_f32=want_f32),
        grid=(N // tn, M // tm),
        in_specs=[pl.BlockSpec((tm, K), lambda j, i: (i, 0)),
                  pl.BlockSpec((None, K, tn), lambda j, i: (l, 0, j))],
        out_specs=[spec] * len(dtypes), out_shape=[jax.ShapeDtypeStruct((M, N), d) for d in dtypes],
        compiler_params=_cparams(("arbitrary", "arbitrary")),
        name="proj",
    )(x, w3)


F32_GROUPS = (1, 2, 4, 5)


def _in_proj_kernel(x_ref, w_ref, *refs, scales):
    dk_ref, dv_ref, fk_ref, fv_ref, zb_ref = refs[-5:]
    f32_out = dict(zip(F32_GROUPS, (dk_ref, dv_ref, fk_ref, fv_ref)))
    j = pl.program_id(1)

    def group(g):
        acc = jnp.dot(x_ref[...], w_ref[...], preferred_element_type=F32)
        tm = acc.shape[0]
        if g in f32_out:
            f32_out[g][...] = acc.reshape(tm, N_HEADS, HEAD_W)
        y = (acc * scales[g]).astype(BF16) if scales[g] != 1.0 else acc.astype(BF16)
        for hh in range(N_HEADS):
            zb_ref[hh] = y[:, hh * HEAD_W:(hh + 1) * HEAD_W]

    for g in range(len(scales)):
        pl.when(j == g)(functools.partial(group, g))


def _in_proj(x, w3, l, tm, scales, prev=None):
    M, K = x.shape
    depth = w3.shape[0]
    W = N_HEADS * HEAD_W
    f32_spec = pl.BlockSpec((None, tm, N_HEADS, HEAD_W), lambda i, j: (l, i, 0, 0))
    f32_shape = jax.ShapeDtypeStruct((depth, M, N_HEADS, HEAD_W), F32)
    prev = list(prev) if prev is not None else []
    return pl.pallas_call(
        functools.partial(_in_proj_kernel, scales=tuple(scales)),
        grid=(M // tm, len(scales)),
        in_specs=[pl.BlockSpec((tm, K), lambda i, j: (i, 0)),
                  pl.BlockSpec((None, K, W), lambda i, j: (l, 0, j))]
                 + [pl.BlockSpec(memory_space=pl.ANY)] * len(prev),
        out_specs=[f32_spec] * 4 + [pl.BlockSpec((N_HEADS, tm, HEAD_W), lambda i, j: (j, i, 0))],
        out_shape=[f32_shape] * 4 + [jax.ShapeDtypeStruct((len(scales) * N_HEADS, M, HEAD_W), BF16)],
        input_output_aliases={2 + k: k for k in range(len(prev))},
        compiler_params=_cparams(("arbitrary", "arbitrary")),
        name="in_proj",
    )(x, w3, *prev)


def _logit_kernel(x_ref, w_ref, b_ref, lf_ref, *rest, n_valid, with_cumsum):
    fl = jnp.dot(x_ref[...], w_ref[...], preferred_element_type=F32) + b_ref[...]
    lf = jnp.minimum(fl, 0.0) - jnp.log1p(jnp.exp(-jnp.abs(fl)))
    tm = lf.shape[0]
    lane = lax.broadcasted_iota(jnp.int32, lf.shape, 1)
    lf = jnp.where(lane < n_valid, lf, 0.0)
    lf_ref[...] = lf
    if with_cumsum:
        ct_ref, carry = rest

        @pl.when(pl.program_id(0) == 0)
        def _():
            carry[...] = jnp.zeros_like(carry)

        row = lax.broadcasted_iota(jnp.int32, (tm, tm), 0)
        col = lax.broadcasted_iota(jnp.int32, (tm, tm), 1)
        tri = (row >= col).astype(BF16)
        hi, mid, lo = _split3(lf)
        c = (jnp.dot(tri, hi, preferred_element_type=F32)
             + jnp.dot(tri, mid, preferred_element_type=F32)
             + jnp.dot(tri, lo, preferred_element_type=F32)) + carry[...]
        ct_ref[...] = c.T[:ct_ref.shape[0], :]
        carry[...] = c[tm - 1:tm, :]


def _logits(x, wl3, bl3, l, tm, with_cumsum):
    M, K = x.shape
    out_shape = [jax.ShapeDtypeStruct((M, LANES), F32)]
    out_specs = [pl.BlockSpec((tm, LANES), lambda i: (i, 0))]
    scratch = []
    if with_cumsum:
        out_shape += [jax.ShapeDtypeStruct((N_HEADS, M), F32)]
        out_specs += [pl.BlockSpec((N_HEADS, tm), lambda i: (0, i))]
        scratch = [pltpu.VMEM((1, LANES), F32)]
    return pl.pallas_call(
        functools.partial(_logit_kernel, n_valid=N_HEADS, with_cumsum=with_cumsum),
        grid=(M // tm,),
        in_specs=[pl.BlockSpec((tm, K), lambda i: (i, 0)),
                  pl.BlockSpec((None, K, LANES), lambda i: (l, 0, 0)),
                  pl.BlockSpec((None, 1, LANES), lambda i: (l, 0, 0))],
        out_specs=out_specs, out_shape=out_shape, scratch_shapes=scratch,
        compiler_params=_cparams(("arbitrary",)),
        name="logits",
    )(x, wl3, bl3)


def _flash_init(m_sc, l_sc, acc_sc):
    m_sc[...] = jnp.full_like(m_sc, NEG)
    l_sc[...] = jnp.zeros_like(l_sc)
    acc_sc[...] = jnp.zeros_like(acc_sc)


def _flash_update(s, v, m_sc, l_sc, acc_sc):
    n = s.shape[1] // LANES
    m_prev = m_sc[...]
    m_new = jnp.maximum(m_prev, jnp.max(s, axis=-1, keepdims=True))
    alpha = jnp.exp2(m_prev - m_new)
    p = jnp.exp2(s - jnp.concatenate([m_new] * n, axis=1))
    psum = p[:, :LANES]
    for c in range(1, n):
        psum = psum + p[:, c * LANES:(c + 1) * LANES]
    l_sc[...] = alpha * l_sc[...] + psum
    acc_sc[...] = alpha * acc_sc[...] + jnp.dot(p.astype(BF16), v, preferred_element_type=F32)
    m_sc[...] = m_new


def _flash_rowsum(l_sc):
    return jnp.sum(l_sc[...], axis=-1, keepdims=True)


def _diff_attn_kernel(q_ref, k_ref, v_ref, bd_ref, bs_ref, lam_ref, g_ref, o_ref,
                      m_sc, l_sc, acc_sc, *, blk, lam0):
    qi = pl.program_id(1)
    q = q_ref[...].astype(F32)
    lane = lax.broadcasted_iota(jnp.int32, q.shape, 1)
    qq = jnp.concatenate([jnp.where(lane < DH_DIFF, q, 0.0), jnp.where(lane >= DH_DIFF, q, 0.0)], axis=0).astype(BF16)
    _flash_init(m_sc, l_sc, acc_sc)

    def step(kj, bias_ref):
        ks = pl.multiple_of(kj * blk, blk)
        k = k_ref[pl.ds(ks, blk), :]
        v = v_ref[pl.ds(ks, blk), :]
        s = lax.dot_general(qq, k, NT_DIMS, preferred_element_type=F32)
        if bias_ref is not None:
            b = bias_ref[...]
            s = s + jnp.concatenate([b, b], axis=0)
        _flash_update(s, v, m_sc, l_sc, acc_sc)

    def far_body(kj, carry):
        step(kj, None)
        return carry

    lax.fori_loop(0, jnp.maximum(qi - 1, 0), far_body, 0)

    @pl.when(qi >= 1)
    def _():
        step(qi - 1, bs_ref)

    step(qi, bd_ref)

    lam = _lam_value(lam_ref[...], lam0)
    o = acc_sc[...] / _flash_rowsum(l_sc)
    o = o[:blk] - lam * o[blk:]
    o = o * lax.rsqrt(jnp.mean(o * o, axis=-1, keepdims=True) + LN_EPS)
    o_ref[...] = (o * g_ref[...] * (1.0 - lam0)).astype(o_ref.dtype)


def _diff_attention(zb, bias_diag, bias_sub, lam_p, g, l, lam0, blk):
    T = zb.shape[1]
    H = N_HEADS
    return pl.pallas_call(
        functools.partial(_diff_attn_kernel, blk=blk, lam0=lam0),
        grid=(H, T // blk),
        in_specs=[pl.BlockSpec((None, blk, LANES), lambda h, i: (h, i, 0)),
                  pl.BlockSpec((None, T, LANES), lambda h, i: (H + h, 0, 0)),
                  pl.BlockSpec((None, T, LANES), lambda h, i: (2 * H + h, 0, 0)),
                  pl.BlockSpec((None, blk, blk), lambda h, i: (h, 0, 0)),
                  pl.BlockSpec((None, blk, blk), lambda h, i: (h, 0, 0)),
                  pl.BlockSpec((None, 4, DH_DIFF), lambda h, i: (l, 0, 0)),
                  pl.BlockSpec((None, 1, LANES), lambda h, i: (l, 0, 0))],
        out_specs=pl.BlockSpec((None, blk, LANES), lambda h, i: (h, i, 0)),
        out_shape=jax.ShapeDtypeStruct((H, T, LANES), BF16),
        scratch_shapes=[pltpu.VMEM((2 * blk, LANES), F32), pltpu.VMEM((2 * blk, LANES), F32),
                        pltpu.VMEM((2 * blk, LANES), F32)],
        compiler_params=_cparams(("arbitrary", "arbitrary")),
        name="diff_attn",
    )(zb, zb, zb, bias_diag, bias_sub, lam_p, g)


def _fox_attn_kernel(q_ref, k_ref, v_ref, ct_ref, o_ref, m_sc, l_sc, acc_sc, *, blk):
    h = pl.program_id(0)
    qi = pl.program_id(1)
    q = q_ref[...]
    qs = pl.multiple_of(qi * blk, blk)
    c_ref0 = ct_ref[pl.ds(h, 1), pl.ds(qs, blk)][:, 0:1]
    _flash_init(m_sc, l_sc, acc_sc)

    def step(kj, masked):
        ks = pl.multiple_of(kj * blk, blk)
        k = k_ref[pl.ds(ks, blk), :]
        v = v_ref[pl.ds(ks, blk), :]
        ck = (ct_ref[pl.ds(h, 1), pl.ds(ks, blk)] - c_ref0) * LOG2E
        s = lax.dot_general(q, k, NT_DIMS, preferred_element_type=F32) - ck
        if masked:
            row = lax.broadcasted_iota(jnp.int32, s.shape, 0)
            col = lax.broadcasted_iota(jnp.int32, s.shape, 1)
            s = jnp.where(row >= col, s, NEG)
        _flash_update(s, v, m_sc, l_sc, acc_sc)

    def far_body(kj, carry):
        step(kj, False)
        return carry

    lax.fori_loop(0, qi, far_body, 0)
    step(qi, True)
    o_ref[...] = (acc_sc[...] / _flash_rowsum(l_sc)).astype(o_ref.dtype)


def _fox_attention(zb, ct, blk):
    T = zb.shape[1]
    H = N_HEADS
    return pl.pallas_call(
        functools.partial(_fox_attn_kernel, blk=blk),
        grid=(H, T // blk),
        in_specs=[pl.BlockSpec((None, blk, LANES), lambda h, i: (3 * H + h, i, 0)),
                  pl.BlockSpec((None, T, LANES), lambda h, i: (4 * H + h, 0, 0)),
                  pl.BlockSpec((None, T, LANES), lambda h, i: (5 * H + h, 0, 0)),
                  pl.BlockSpec((N_HEADS, T), lambda h, i: (0, 0))],
        out_specs=pl.BlockSpec((None, blk, LANES), lambda h, i: (h, i, 0)),
        out_shape=jax.ShapeDtypeStruct((H, T, LANES), BF16),
        scratch_shapes=[pltpu.VMEM((blk, LANES), F32), pltpu.VMEM((blk, LANES), F32),
                        pltpu.VMEM((blk, LANES), F32)],
        compiler_params=_cparams(("arbitrary", "arbitrary")),
        name="fox_attn",
    )(zb, zb, zb, ct)


def _mm_pn_kernel(*refs, n_parts, alpha, tiled):
    a_refs = refs[:n_parts]
    w_ref, r_ref, g_ref, b_ref, of_ref, ob_ref = refs[n_parts:n_parts + 6]

    def finish(y):
        y = _post_norm(alpha * r_ref[...] + y, g_ref[...], b_ref[...])
        of_ref[...] = y
        ob_ref[...] = y.astype(BF16)

    if not tiled:
        y = None
        row = 0
        for a_ref in a_refs:
            a = a_ref[...]
            if a.ndim == 3:
                a = _lane_cat(a)
            d = jnp.dot(a, w_ref[row:row + a.shape[1], :], preferred_element_type=F32)
            y = d if y is None else y + d
            row += a.shape[1]
        finish(y)
        return

    acc = refs[n_parts + 6]
    k = pl.program_id(1)

    @pl.when(k == 0)
    def _():
        acc[...] = jnp.zeros_like(acc)

    acc[...] += jnp.dot(a_refs[0][...], w_ref[...], preferred_element_type=F32)

    @pl.when(k == pl.num_programs(1) - 1)
    def _():
        finish(acc[...])


def _mm_post_norm(parts, w3, l, resid, g3, b3, ln_idx, alpha, tm, tk=None):
    M, D = resid.shape
    tiled = tk is not None
    in_specs = []
    if tiled:
        assert len(parts) == 1 and parts[0].ndim == 2 and parts[0].shape[1] % tk == 0
        nk = parts[0].shape[1] // tk
        in_specs.append(pl.BlockSpec((tm, tk), lambda i, k: (i, k)))
        in_specs.append(pl.BlockSpec((None, tk, D), lambda i, k: (l, k, 0)))
    else:
        nk = 1
        for a in parts:
            if a.ndim == 3:
                in_specs.append(pl.BlockSpec((a.shape[0], tm, LANES), lambda i, k: (0, i, 0)))
            else:
                in_specs.append(pl.BlockSpec((tm, a.shape[1]), lambda i, k: (i, 0)))
        in_specs.append(pl.BlockSpec((None, w3.shape[1], D), lambda i, k: (l, 0, 0)))
    in_specs += [pl.BlockSpec((tm, D), lambda i, k: (i, 0)),
                 pl.BlockSpec((None, None, 1, D), lambda i, k: (l, ln_idx, 0, 0)),
                 pl.BlockSpec((None, None, 1, D), lambda i, k: (l, ln_idx, 0, 0))]
    return pl.pallas_call(
        functools.partial(_mm_pn_kernel, n_parts=len(parts), alpha=alpha, tiled=tiled),
        grid=(M // tm, nk),
        in_specs=in_specs,
        out_specs=[pl.BlockSpec((tm, D), lambda i, k: (i, 0)), pl.BlockSpec((tm, D), lambda i, k: (i, 0))],
        out_shape=[jax.ShapeDtypeStruct((M, D), F32), jax.ShapeDtypeStruct((M, D), BF16)],
        scratch_shapes=[pltpu.VMEM((tm, D), F32)] if tiled else [],
        compiler_params=_cparams(("arbitrary", "arbitrary")),
        name="mm_post_norm",
    )(*parts, w3, resid, g3, b3)


def _mem_attn_kernel(x_ref, wq_ref, mk_ref, mv_ref, wo_ref, r_ref, g_ref, b_ref, of_ref, ob_ref, *, alpha):
    scale = HEAD_W ** -0.5
    q = (jnp.dot(x_ref[...], wq_ref[...], preferred_element_type=F32) * scale).astype(BF16)
    mk = mk_ref[...]
    mv = mv_ref[...]
    outs = []
    for h in range(N_HEADS_MEM):
        sl = slice(h * HEAD_W, (h + 1) * HEAD_W)
        s = lax.dot_general(q[:, sl], mk[:, sl], NT_DIMS, preferred_element_type=F32)
        s = s - jnp.max(s, axis=-1, keepdims=True)
        p = jnp.exp(s)
        o = jnp.dot(p.astype(BF16), mv[:, sl], preferred_element_type=F32)
        outs.append((o / jnp.sum(p, axis=-1, keepdims=True)).astype(BF16))
    o = jnp.concatenate(outs, axis=1)
    y = jnp.dot(o, wo_ref[...], preferred_element_type=F32)
    y = _post_norm(alpha * r_ref[...] + y, g_ref[...], b_ref[...])
    of_ref[...] = y
    ob_ref[...] = y.astype(BF16)


def _mem_attention(xb, w_mq, mkb, mvb, w_mo, l, resid, g3, b3, alpha, tm):
    M, D = resid.shape
    W = w_mq.shape[2]
    NM = mkb.shape[0]
    return pl.pallas_call(
        functools.partial(_mem_attn_kernel, alpha=alpha),
        grid=(M // tm,),
        in_specs=[pl.BlockSpec((tm, D), lambda i: (i, 0)),
                  pl.BlockSpec((None, D, W), lambda i: (l, 0, 0)),
                  pl.BlockSpec((NM, W), lambda i: (0, 0)),
                  pl.BlockSpec((NM, W), lambda i: (0, 0)),
                  pl.BlockSpec((None, W, D), lambda i: (l, 0, 0)),
                  pl.BlockSpec((tm, D), lambda i: (i, 0)),
                  pl.BlockSpec((None, None, 1, D), lambda i: (l, 1, 0, 0)),
                  pl.BlockSpec((None, None, 1, D), lambda i: (l, 1, 0, 0))],
        out_specs=[pl.BlockSpec((tm, D), lambda i: (i, 0)), pl.BlockSpec((tm, D), lambda i: (i, 0))],
        out_shape=[jax.ShapeDtypeStruct((M, D), F32), jax.ShapeDtypeStruct((M, D), BF16)],
        compiler_params=_cparams(("arbitrary",)),
        name="mem_attn",
    )(xb, w_mq, mkb, mvb, w_mo, resid, g3, b3)


def _swiglu_kernel(x_ref, wg_ref, wu_ref, o_ref):
    x = x_ref[...]
    g = jnp.dot(x, wg_ref[...], preferred_element_type=F32)
    u = jnp.dot(x, wu_ref[...], preferred_element_type=F32)
    o_ref[...] = (g * jax.nn.sigmoid(g) * u).astype(o_ref.dtype)


def _swiglu_up(xb, w_ff_in, l, tm, tn):
    M, D = xb.shape
    dff = w_ff_in.shape[2] // 2
    nj = dff // tn
    return pl.pallas_call(
        _swiglu_kernel,
        grid=(nj, M // tm),
        in_specs=[pl.BlockSpec((tm, D), lambda j, i: (i, 0)),
                  pl.BlockSpec((None, D, tn), lambda j, i: (l, 0, j)),
                  pl.BlockSpec((None, D, tn), lambda j, i: (l, 0, j + nj))],
        out_specs=pl.BlockSpec((tm, tn), lambda j, i: (i, j)),
        out_shape=jax.ShapeDtypeStruct((M, dff), BF16),
        compiler_params=_cparams(("arbitrary", "arbitrary")),
        name="swiglu_up",
    )(xb, w_ff_in, w_ff_in)


def _head_mask(rows, width, row_mod):
    r = lax.broadcasted_iota(jnp.int32, (rows, width), 0)
    c = lax.broadcasted_iota(jnp.int32, (rows, width), 1)
    return jnp.right_shift(c, 7) == jnp.bitwise_and(r, row_mod - 1)


def _lane_suffix_tile(x):
    row = lax.broadcasted_iota(jnp.int32, x.shape, 0)
    lane = lax.broadcasted_iota(jnp.int32, x.shape, 1)
    y = x
    z = x
    for sh in (8, 16, 32, 64):
        shifted = pltpu.roll(y, LANES - sh, 1)
        y = y + jnp.where(lane + sh < LANES, shifted, 0.0)
        z = z + pltpu.roll(z, sh, 1)
    a = z
    for sh in (1, 2, 4):
        shifted = pltpu.roll(a, 8 - sh, 0)
        a = a + jnp.where(row + sh < 8, shifted, 0.0)
    return (y - x) + (a - z), a[0:1, :]


def _decode_kernel(pt_ref, qd_ref, qf_ref, kdn_ref, vdn_ref, kfn_ref, vfn_ref, lfn_ref,
                   blast_ref, bnew_ref, lam_ref, g_ref, *refs, n_pages, lam0):
    P = n_pages
    kd, vd, kf, vf, lfp = (refs[i * P:(i + 1) * P] for i in range(5))
    od_ref, of_ref = refs[5 * P:5 * P + 2]
    qd_sc, qf_sc, md, ld, accd, mf, lf_sum, accf, carry = refs[5 * P + 2:]
    c = pl.program_id(1)
    H = N_HEADS
    PS = kd[0].shape[0]
    R = PS * H

    @pl.when(c == 0)
    def _():
        q = qd_ref[...].astype(F32)
        lane = lax.broadcasted_iota(jnp.int32, q.shape, 1)
        qq = jnp.concatenate([jnp.where(lane < DH_DIFF, q, 0.0), jnp.where(lane >= DH_DIFF, q, 0.0)], axis=0)
        qd_sc[...] = qq.astype(BF16)
        qf_sc[...] = qf_ref[...]
        kn = kdn_ref[...].astype(BF16).astype(F32)
        m0 = jnp.sum(qq * jnp.concatenate([kn, kn], axis=0), axis=-1, keepdims=True) + bnew_ref[...][:, 0:1]
        md[...] = jnp.broadcast_to(m0, md.shape)
        ld[...] = jnp.full_like(ld, 1.0 / LANES)
        vn = vdn_ref[...].astype(BF16).astype(F32)
        accd[...] = jnp.concatenate([vn, vn], axis=0)
        m0 = jnp.sum(qf_ref[...].astype(F32) * kfn_ref[...].astype(BF16).astype(F32), axis=-1, keepdims=True)
        mf[...] = jnp.broadcast_to(m0, mf.shape)
        lf_sum[...] = jnp.full_like(lf_sum, 1.0 / LANES)
        accf[...] = vfn_ref[...].astype(BF16).astype(F32)
        lane1 = lax.broadcasted_iota(jnp.int32, (1, LANES), 1)
        v = jnp.where(lane1 < H, lfn_ref[...], 0.0)
        for sh in (8, 16, 32, 64):
            v = v + pltpu.roll(v, sh, 1)
        carry[...] = v

    def own_head(rows):
        r = lax.broadcasted_iota(jnp.int32, (rows, R), 0)
        col = lax.broadcasted_iota(jnp.int32, (rows, R), 1)
        return jnp.bitwise_and(col, H - 1) == jnp.bitwise_and(r, H - 1)

    own_d = own_head(2 * H)
    own_f = own_head(H)
    is_last = (c == 0).astype(F32)

    dec = [None] * P
    cv = carry[...]
    for r in range(P - 1, -1, -1):
        inner, tot = _lane_suffix_tile(lfp[r][...])
        tile = (inner + cv) * LOG2E
        dec[r] = jnp.concatenate([tile[i:i + 1, :] for i in range(8)], axis=1)
        cv = cv + tot
    carry[...] = cv

    sd, sf = [], []
    for r in range(P):
        s = lax.dot_general(qd_sc[...], kd[r][...].reshape(R, HEAD_W).astype(BF16), NT_DIMS,
                            preferred_element_type=F32)
        if r == P - 1:
            s = s + is_last * blast_ref[...]
        sd.append(jnp.where(own_d, s, NEG))
        s = lax.dot_general(qf_sc[...], kf[r][...].reshape(R, HEAD_W).astype(BF16), NT_DIMS,
                            preferred_element_type=F32) + dec[r]
        sf.append(jnp.where(own_f, s, NEG))

    def update(s_pages, v_refs, m_sc, l_sc, acc_sc):
        m_prev = m_sc[...]
        m_cur = s_pages[0]
        for s in s_pages[1:]:
            m_cur = jnp.maximum(m_cur, s)
        m_new = jnp.maximum(m_prev, jnp.max(m_cur, axis=-1, keepdims=True))
        alpha = jnp.exp2(m_prev - m_new)
        m_sc[...] = m_new
        m_wide = jnp.concatenate([m_new] * (R // LANES), axis=1)
        acc = alpha * acc_sc[...]
        lsum = alpha * l_sc[...]
        for s, v_ref in zip(s_pages, v_refs):
            p = jnp.exp2(s - m_wide)
            for i in range(R // LANES):
                lsum = lsum + p[:, i * LANES:(i + 1) * LANES]
            acc = acc + jnp.dot(p.astype(BF16), v_ref[...].reshape(R, HEAD_W).astype(BF16),
                                preferred_element_type=F32)
        l_sc[...] = lsum
        acc_sc[...] = acc

    update(sd, vd, md, ld, accd)
    update(sf, vf, mf, lf_sum, accf)

    @pl.when(c == pl.num_programs(1) - 1)
    def _():
        lam = _lam_value(lam_ref[...], lam0)
        o = accd[...] / jnp.sum(ld[...], axis=-1, keepdims=True)
        o = o[:H] - lam * o[H:]
        o = o * lax.rsqrt(jnp.mean(o * o, axis=-1, keepdims=True) + LN_EPS)
        od_ref[...] = (o * g_ref[...] * (1.0 - lam0)).astype(od_ref.dtype)
        of_ref[...] = (accf[...] / jnp.sum(lf_sum[...], axis=-1, keepdims=True)).astype(of_ref.dtype)


def _decode_attention(page_table, zsb, new_kv, lf_new, caches, bias_last, bias_new, lam_p, g, l, lam0, n_pages):
    B = zsb.shape[0]
    H = N_HEADS
    ckd, cvd, ckf, cvf, clf = caches
    PS = ckd.shape[2]
    per_seq = page_table.shape[1]
    nch = per_seq // n_pages

    def group_spec(g_idx):
        return pl.BlockSpec((None, H, HEAD_W), lambda b, c, pt: (b, g_idx, 0))

    new_spec = pl.BlockSpec((None, None, H, HEAD_W), lambda b, c, pt: (l, b, 0, 0))
    in_specs = [group_spec(0), group_spec(3), new_spec, new_spec, new_spec, new_spec,
                pl.BlockSpec((None, 1, LANES), lambda b, c, pt: (b, 0, 0)),
                pl.BlockSpec((1, PS * H), lambda b, c, pt: (0, 0)),
                pl.BlockSpec((2 * H, LANES), lambda b, c, pt: (0, 0)),
                pl.BlockSpec((None, 4, DH_DIFF), lambda b, c, pt: (l, 0, 0)),
                pl.BlockSpec((None, 1, HEAD_W), lambda b, c, pt: (l, 0, 0))]
    args = [zsb, zsb, *new_kv, lf_new.reshape(B, 1, LANES), bias_last, bias_new, lam_p, g]
    for cache in (ckd, cvd, ckf, cvf):
        for r in range(n_pages):
            in_specs.append(pl.BlockSpec((None, None, PS, H, HEAD_W),
                                         lambda b, c, pt, r=r: (l, pt[b, (nch - 1 - c) * n_pages + r], 0, 0, 0)))
            args.append(cache)
    for r in range(n_pages):
        in_specs.append(pl.BlockSpec((None, None, 8, LANES),
                                     lambda b, c, pt, r=r: (l, pt[b, (nch - 1 - c) * n_pages + r], 0, 0)))
        args.append(clf)
    out_spec = pl.BlockSpec((None, H, HEAD_W), lambda b, c, pt: (b, 0, 0))
    grid_spec = pltpu.PrefetchScalarGridSpec(
        num_scalar_prefetch=1, grid=(B, nch), in_specs=in_specs, out_specs=[out_spec, out_spec],
        scratch_shapes=[pltpu.VMEM((2 * H, HEAD_W), BF16), pltpu.VMEM((H, HEAD_W), BF16),
                        pltpu.VMEM((2 * H, LANES), F32), pltpu.VMEM((2 * H, LANES), F32),
                        pltpu.VMEM((2 * H, HEAD_W), F32),
                        pltpu.VMEM((H, LANES), F32), pltpu.VMEM((H, LANES), F32), pltpu.VMEM((H, HEAD_W), F32),
                        pltpu.VMEM((1, LANES), F32)])
    return pl.pallas_call(
        functools.partial(_decode_kernel, n_pages=n_pages, lam0=lam0),
        grid_spec=grid_spec,
        out_shape=[jax.ShapeDtypeStruct((B, H, HEAD_W), BF16), jax.ShapeDtypeStruct((B, H, HEAD_W), BF16)],
        compiler_params=_cparams(("arbitrary", "arbitrary")),
        name="decode_attn",
    )(page_table, *args)


def _mem_decode_kernel(q_ref, mk_ref, mv_ref, o_ref):
    H8 = 8
    W = q_ref.shape[-1]
    mask = _head_mask(H8, W, H8)
    q = jnp.broadcast_to(q_ref[...].astype(F32), (H8, W))
    qbd = jnp.where(mask, q, 0.0).astype(BF16)
    s = lax.dot_general(qbd, mk_ref[...].astype(BF16), NT_DIMS, preferred_element_type=F32)
    s = s - jnp.max(s, axis=-1, keepdims=True)
    p = jnp.exp(s)
    o = jnp.dot(p.astype(BF16), mv_ref[...].astype(BF16), preferred_element_type=F32)
    o = jnp.where(mask, o / jnp.sum(p, axis=-1, keepdims=True), 0.0)
    o_ref[...] = jnp.sum(o, axis=0, keepdims=True).astype(o_ref.dtype)


def _mem_decode_attention(qb, cmk, cmv, l):
    B, W = qb.shape
    NM = cmk.shape[2]
    out = pl.pallas_call(
        _mem_decode_kernel,
        grid=(B,),
        in_specs=[pl.BlockSpec((None, 1, W), lambda b: (b, 0, 0)),
                  pl.BlockSpec((None, None, NM, W), lambda b: (l, b, 0, 0)),
                  pl.BlockSpec((None, None, NM, W), lambda b: (l, b, 0, 0))],
        out_specs=pl.BlockSpec((None, 1, W), lambda b: (b, 0, 0)),
        out_shape=jax.ShapeDtypeStruct((B, 1, W), BF16),
        compiler_params=_cparams(("arbitrary",)),
        name="mem_decode_attn",
    )(qb.reshape(B, 1, W), cmk, cmv)
    return out.reshape(B, W)


def _bias_by_distance(rel_bias, dist):
    bucket = _rel_bucket(dist)
    rel = (rel_bias - rel_bias[N_BUCKETS - 1]) * LOG2E
    out = jnp.zeros((rel_bias.shape[1],) + dist.shape, F32)
    for b in range(N_BUCKETS - 1):
        out = jnp.where((bucket == b)[None], rel[b].reshape((-1,) + (1,) * dist.ndim), out)
    return out


def _bias_tables(rel_bias, blk):
    assert blk >= MAX_DISTANCE
    r = jnp.arange(blk, dtype=jnp.int32)[:, None]
    c = jnp.arange(blk, dtype=jnp.int32)[None, :]
    diag = jnp.where((r >= c)[None], _bias_by_distance(rel_bias, jnp.maximum(r - c, 0)), NEG)
    sub = _bias_by_distance(rel_bias, blk + r - c)
    return diag, sub


def _decode_bias(rel_bias, page_size):
    assert page_size >= MAX_DISTANCE
    last = _bias_by_distance(rel_bias, page_size - jnp.arange(page_size, dtype=jnp.int32))
    new = jnp.broadcast_to(_bias_by_distance(rel_bias, jnp.zeros((1,), jnp.int32)), (N_HEADS, LANES))
    return last.T.reshape(1, page_size * N_HEADS), jnp.concatenate([new, new], axis=0)


def kernel(x_prompt, x_sample, cache_diff_k, cache_diff_v, cache_fox_k, cache_fox_v, cache_fox_logf,
           cache_mem_k, cache_mem_v, page_table, mem_prompt, w_in, b_forget, diff_lambda, diff_subln_g,
           rel_bias, w_o, w_mq, w_mkv, w_mo, w_ff_in, w_ff_out, ln_g, ln_b):
    depth = w_in.shape[0]
    alpha = (2 * depth) ** 0.25
    _, T, D = x_prompt.shape
    DB = x_sample.shape[0]
    W = N_HEADS * HEAD_W
    n_qkv = 6 * W
    pool, page_size = cache_diff_k.shape[1:3]
    n_mem = mem_prompt.shape[1]
    w_mem = w_mq.shape[2]
    blk = min(ATT_BLK, T)
    blk_fox = 2 * blk if T % (2 * blk) == 0 else blk
    tm = min(512, T)
    assert rel_bias.shape == (N_BUCKETS, N_HEADS) and w_in.shape[2] == n_qkv + N_HEADS

    w_in_b = w_in.astype(BF16)
    w_fl_b = jnp.pad(w_in[:, :, n_qkv:], ((0, 0), (0, 0), (0, LANES - N_HEADS))).astype(BF16)
    b_fl = jnp.pad(b_forget, ((0, 0), (0, LANES - N_HEADS))).reshape(depth, 1, LANES)
    w_o_b, w_mq_b, w_mkv_b, w_mo_b = (w.astype(BF16) for w in (w_o, w_mq, w_mkv, w_mo))
    w_ff_in_b, w_ff_out_b = w_ff_in.astype(BF16), w_ff_out.astype(BF16)
    ln_g4 = ln_g.reshape(depth, 3, 1, D)
    ln_b4 = ln_b.reshape(depth, 3, 1, D)
    g_sub = diff_subln_g.reshape(depth, 1, HEAD_W)
    qk_scales = (DH_DIFF ** -0.5 * LOG2E, 1.0, 1.0, HEAD_W ** -0.5 * LOG2E, 1.0, 1.0)
    bias_diag, bias_sub = _bias_tables(rel_bias, blk)
    bias_last, bias_new = _decode_bias(rel_bias, page_size)
    d_ff = w_ff_out.shape[1]
    tn_ff = 512 if d_ff % 512 == 0 else d_ff
    tk_ff = d_ff // 4 if d_ff % (4 * LANES) == 0 else d_ff

    def ffn(xf, xb, l, tmm):
        hff = _swiglu_up(xb, w_ff_in_b, l, tmm, tn_ff)
        return _mm_post_norm([hff], w_ff_out_b, l, xf, ln_g4, ln_b4, 2, alpha, tmm, tk_ff)

    xf = x_prompt.reshape(T, D)
    xb = xf.astype(BF16)
    memb = mem_prompt.reshape(n_mem, D).astype(BF16)
    p_kv, p_lf, p_mk, p_mv = None, [], [], []
    for l in range(depth):
        lam0 = _lambda_init(l)
        *p_kv, zb = _in_proj(xb, w_in_b, l, tm, qk_scales, p_kv)
        lf, ct = _logits(xb, w_fl_b, b_fl, l, tm, True)
        od = _diff_attention(zb, bias_diag, bias_sub, diff_lambda, g_sub, l, lam0, blk)
        of = _fox_attention(zb, ct, blk_fox)
        xf, xb = _mm_post_norm([od, of], w_o_b, l, xf, ln_g4, ln_b4, 0, alpha, tm)
        mkv_f, mkv_b = _proj(memb, w_mkv_b, l, w_mem, n_mem, want_f32=True)
        xf, xb = _mem_attention(xb, w_mq_b, mkv_b[:, :w_mem], mkv_b[:, w_mem:], w_mo_b, l, xf, ln_g4, ln_b4, alpha, tm)
        xf, xb = ffn(xf, xb, l, tm)
        p_lf.append(lf[:, :N_HEADS])
        p_mk.append(mkv_f[:, :w_mem])
        p_mv.append(mkv_f[:, w_mem:])
    y_prompt = xf.reshape(1, T, D)

    assert page_size * N_HEADS == 8 * LANES
    caches = (cache_diff_k, cache_diff_v, cache_fox_k, cache_fox_v, cache_fox_logf.reshape(depth, pool, 8, LANES))
    cmk = cache_mem_k.reshape(depth, DB, n_mem, w_mem)
    cmv = cache_mem_v.reshape(depth, DB, n_mem, w_mem)
    per_seq = page_table.shape[1]
    n_pages = 8 if per_seq % 8 == 0 else per_seq
    xf = x_sample.reshape(DB, D)
    xb = xf.astype(BF16)
    s_kv, s_lf = None, []
    for l in range(depth):
        lam0 = _lambda_init(l)
        *s_kv, zb = _in_proj(xb, w_in_b, l, DB, qk_scales, s_kv)
        (lf,) = _logits(xb, w_fl_b, b_fl, l, DB, False)
        od, of = _decode_attention(page_table, jnp.swapaxes(zb, 0, 1), s_kv, lf, caches,
                                   bias_last, bias_new, diff_lambda, g_sub, l, lam0, n_pages)
        xf, xb = _mm_post_norm([od.reshape(DB, W), of.reshape(DB, W)], w_o_b, l, xf, ln_g4, ln_b4, 0, alpha, DB)
        (qm,) = _proj(xb, w_mq_b, l, w_mem, DB, scale=HEAD_W ** -0.5)
        om = _mem_decode_attention(qm, cmk, cmv, l)
        xf, xb = _mm_post_norm([om], w_mo_b, l, xf, ln_g4, ln_b4, 1, alpha, DB)
        xf, xb = ffn(xf, xb, l, DB)
        s_lf.append(lf[:, :N_HEADS])
    y_sample = xf.reshape(DB, 1, D)

    def st(parts, *shape):
        return jnp.stack(parts).reshape(depth, *shape)

    return (y_prompt, y_sample,
            *(a.reshape(depth, 1, T, N_HEADS, HEAD_W) for a in p_kv), st(p_lf, 1, T, N_HEADS),
            st(p_mk, 1, n_mem, N_HEADS_MEM, HEAD_W), st(p_mv, 1, n_mem, N_HEADS_MEM, HEAD_W),
            *(a.reshape(depth, DB, 1, N_HEADS, HEAD_W) for a in s_kv), st(s_lf, DB, 1, N_HEADS))
```

```python
import functools
import math

import jax
import jax.numpy as jnp
from jax import lax
from jax.experimental import pallas as pl
from jax.experimental.pallas import tpu as pltpu

F32 = jnp.float32
BF16 = jnp.bfloat16

LANES = 128
LOG2E = 1.4426950408889634
NEG = -1e30
LN_EPS = 1e-5
N_BUCKETS = 32
MAX_DISTANCE = 128
DH_DIFF = 64
HEAD_W = 128
N_HEADS = 8
N_HEADS_MEM = 4
ATT_BLK = 512
VMEM_LIMIT = 56 * 1024 * 1024

NT_DIMS = (((1,), (1,)), ((), ()))


def _cparams(sem):
    return pltpu.CompilerParams(dimension_semantics=sem, vmem_limit_bytes=VMEM_LIMIT)


def _lambda_init(layer):
    return 0.8 - 0.6 * math.exp(-0.3 * layer)


def _rel_bucket(dist):
    n = jnp.maximum(dist, 0)
    max_exact = N_BUCKETS // 2
    nf = jnp.maximum(n, 1).astype(F32)
    large = max_exact + (jnp.log(nf / max_exact) / math.log(MAX_DISTANCE / max_exact)
                         * (N_BUCKETS - max_exact)).astype(jnp.int32)
    large = jnp.minimum(large, N_BUCKETS - 1)
    return jnp.where(n < max_exact, n, large)


def _split3(x):
    hi = x.astype(BF16)
    r1 = x - hi.astype(F32)
    mid = r1.astype(BF16)
    lo = (r1 - mid.astype(F32)).astype(BF16)
    return hi, mid, lo


def _lane_cat(a):
    return jnp.concatenate([a[h] for h in range(a.shape[0])], axis=1)


def _post_norm(h, g, b):
    mu = jnp.mean(h, axis=-1, keepdims=True)
    d = h - mu
    var = jnp.mean(d * d, axis=-1, keepdims=True)
    return d * lax.rsqrt(var + LN_EPS) * g + b


def _lam_value(lp, lam0):
    a = jnp.sum(lp[0:1] * lp[1:2], axis=-1, keepdims=True)
    b = jnp.sum(lp[2:3] * lp[3:4], axis=-1, keepdims=True)
    return jnp.exp(a) - jnp.exp(b) + lam0


def _proj_kernel(x_ref, w_ref, *out_refs, scale, want_f32):
    acc = jnp.dot(x_ref[...], w_ref[...], preferred_element_type=F32)
    if want_f32:
        out_refs[0][...] = acc
    out_refs[-1][...] = (acc * scale).astype(BF16)


def _proj(x, w3, l, tn, tm, scale=1.0, want_f32=False):
    M, K = x.shape
    N = w3.shape[2]
    spec = pl.BlockSpec((tm, tn), lambda j, i: (i, j))
    dtypes = ([F32] if want_f32 else []) + [BF16]
    return pl.pallas_call(
        functools.partial(_proj_kernel, scale=scale, want_f32=want_f32),
        grid=(N // tn, M // tm),
        in_specs=[pl.BlockSpec((tm, K), lambda j, i: (i, 0)),
                  pl.BlockSpec((None, K, tn), lambda j, i: (l, 0, j))],
        out_specs=[spec] * len(dtypes), out_shape=[jax.ShapeDtypeStruct((M, N), d) for d in dtypes],
        compiler_params=_cparams(("arbitrary", "arbitrary")),
        name="proj",
    )(x, w3)


F32_GROUPS = (1, 2, 4, 5)


def _in_proj_kernel(x_ref, w_ref, *refs, scales):
    dk_ref, dv_ref, fk_ref, fv_ref, zb_ref = refs[-5:]
    f32_out = dict(zip(F32_GROUPS, (dk_ref, dv_ref, fk_ref, fv_ref)))
    j = pl.program_id(1)

    def group(g):
        acc = jnp.dot(x_ref[...], w_ref[...], preferred_element_type=F32)
        tm = acc.shape[0]
        if g in f32_out:
            f32_out[g][...] = acc.reshape(tm, N_HEADS, HEAD_W)
        y = (acc * scales[g]).astype(BF16) if scales[g] != 1.0 else acc.astype(BF16)
        for hh in range(N_HEADS):
            zb_ref[hh] = y[:, hh * HEAD_W:(hh + 1) * HEAD_W]

    for g in range(len(scales)):
        pl.when(j == g)(functools.partial(group, g))


def _in_proj(x, w3, l, tm, scales, prev=None):
    M, K = x.shape
    depth = w3.shape[0]
    W = N_HEADS * HEAD_W
    f32_spec = pl.BlockSpec((None, tm, N_HEADS, HEAD_W), lambda i, j: (l, i, 0, 0))
    f32_shape = jax.ShapeDtypeStruct((depth, M, N_HEADS, HEAD_W), F32)
    prev = list(prev) if prev is not None else []
    return pl.pallas_call(
        functools.partial(_in_proj_kernel, scales=tuple(scales)),
        grid=(M // tm, len(scales)),
        in_specs=[pl.BlockSpec((tm, K), lambda i, j: (i, 0)),
                  pl.BlockSpec((None, K, W), lambda i, j: (l, 0, j))]
                 + [pl.BlockSpec(memory_space=pl.ANY)] * len(prev),
        out_specs=[f32_spec] * 4 + [pl.BlockSpec((N_HEADS, tm, HEAD_W), lambda i, j: (j, i, 0))],
        out_shape=[f32_shape] * 4 + [jax.ShapeDtypeStruct((len(scales) * N_HEADS, M, HEAD_W), BF16)],
        input_output_aliases={2 + k: k for k in range(len(prev))},
        compiler_params=_cparams(("arbitrary", "arbitrary")),
        name="in_proj",
    )(x, w3, *prev)


def _logit_kernel(x_ref, w_ref, b_ref, lf_ref, *rest, n_valid, with_cumsum):
    fl = jnp.dot(x_ref[...], w_ref[...], preferred_element_type=F32) + b_ref[...]
    lf = jnp.minimum(fl, 0.0) - jnp.log1p(jnp.exp(-jnp.abs(fl)))
    tm = lf.shape[0]
    lane = lax.broadcasted_iota(jnp.int32, lf.shape, 1)
    lf = jnp.where(lane < n_valid, lf, 0.0)
    lf_ref[...] = lf
    if with_cumsum:
        ct_ref, carry = rest

        @pl.when(pl.program_id(0) == 0)
        def _():
            carry[...] = jnp.zeros_like(carry)

        row = lax.broadcasted_iota(jnp.int32, (tm, tm), 0)
        col = lax.broadcasted_iota(jnp.int32, (tm, tm), 1)
        tri = (row >= col).astype(BF16)
        hi, mid, lo = _split3(lf)
        c = (jnp.dot(tri, hi, preferred_element_type=F32)
             + jnp.dot(tri, mid, preferred_element_type=F32)
             + jnp.dot(tri, lo, preferred_element_type=F32)) + carry[...]
        ct_ref[...] = c.T[:ct_ref.shape[0], :]
        carry[...] = c[tm - 1:tm, :]


def _logits(x, wl3, bl3, l, tm, with_cumsum):
    M, K = x.shape
    out_shape = [jax.ShapeDtypeStruct((M, LANES), F32)]
    out_specs = [pl.BlockSpec((tm, LANES), lambda i: (i, 0))]
    scratch = []
    if with_cumsum:
        out_shape += [jax.ShapeDtypeStruct((N_HEADS, M), F32)]
        out_specs += [pl.BlockSpec((N_HEADS, tm), lambda i: (0, i))]
        scratch = [pltpu.VMEM((1, LANES), F32)]
    return pl.pallas_call(
        functools.partial(_logit_kernel, n_valid=N_HEADS, with_cumsum=with_cumsum),
        grid=(M // tm,),
        in_specs=[pl.BlockSpec((tm, K), lambda i: (i, 0)),
                  pl.BlockSpec((None, K, LANES), lambda i: (l, 0, 0)),
                  pl.BlockSpec((None, 1, LANES), lambda i: (l, 0, 0))],
        out_specs=out_specs, out_shape=out_shape, scratch_shapes=scratch,
        compiler_params=_cparams(("arbitrary",)),
        name="logits",
    )(x, wl3, bl3)


def _flash_init(m_sc, l_sc, acc_sc):
    m_sc[...] = jnp.full_like(m_sc, NEG)
    l_sc[...] = jnp.zeros_like(l_sc)
    acc_sc[...] = jnp.zeros_like(acc_sc)


def _flash_update(s, v, m_sc, l_sc, acc_sc, bf16_exp=False):
    n = s.shape[1] // LANES
    m_prev = m_sc[...]
    m_new = jnp.maximum(m_prev, jnp.max(s, axis=-1, keepdims=True))
    alpha = jnp.exp2(m_prev - m_new)
    z = s - jnp.concatenate([m_new] * n, axis=1)
    p = jnp.exp2(z.astype(BF16)) if bf16_exp else jnp.exp2(z)
    psum = p[:, :LANES]
    for c in range(1, n):
        psum = psum + p[:, c * LANES:(c + 1) * LANES]
    l_sc[...] = alpha * l_sc[...] + psum.astype(F32)
    acc_sc[...] = alpha * acc_sc[...] + jnp.dot(p.astype(BF16), v, preferred_element_type=F32)
    m_sc[...] = m_new


def _flash_rowsum(l_sc):
    return jnp.sum(l_sc[...], axis=-1, keepdims=True)


def _diff_attn_kernel(q_ref, k_ref, v_ref, bd_ref, bs_ref, lam_ref, g_ref, o_ref,
                      m_sc, l_sc, acc_sc, *, blk, lam0, bf16_exp):
    qi = pl.program_id(1)
    q = q_ref[...].astype(F32)
    lane = lax.broadcasted_iota(jnp.int32, q.shape, 1)
    qq = jnp.concatenate([jnp.where(lane < DH_DIFF, q, 0.0), jnp.where(lane >= DH_DIFF, q, 0.0)], axis=0).astype(BF16)
    _flash_init(m_sc, l_sc, acc_sc)

    def step(kj, bias_ref):
        ks = pl.multiple_of(kj * blk, blk)
        k = k_ref[pl.ds(ks, blk), :]
        v = v_ref[pl.ds(ks, blk), :]
        s = lax.dot_general(qq, k, NT_DIMS, preferred_element_type=F32)
        if bias_ref is not None:
            b = bias_ref[...]
            s = s + jnp.concatenate([b, b], axis=0)
        _flash_update(s, v, m_sc, l_sc, acc_sc, bf16_exp)

    def far_body(kj, carry):
        step(kj, None)
        return carry

    lax.fori_loop(0, jnp.maximum(qi - 1, 0), far_body, 0)

    @pl.when(qi >= 1)
    def _():
        step(qi - 1, bs_ref)

    step(qi, bd_ref)

    lam = _lam_value(lam_ref[...], lam0)
    o = acc_sc[...] / _flash_rowsum(l_sc)
    o = o[:blk] - lam * o[blk:]
    o = o * lax.rsqrt(jnp.mean(o * o, axis=-1, keepdims=True) + LN_EPS)
    o_ref[...] = (o * g_ref[...] * (1.0 - lam0)).astype(o_ref.dtype)


def _diff_attention(zb, bias_diag, bias_sub, lam_p, g, l, lam0, blk, bf16_exp=False):
    T = zb.shape[1]
    H = N_HEADS
    return pl.pallas_call(
        functools.partial(_diff_attn_kernel, blk=blk, lam0=lam0, bf16_exp=bf16_exp),
        grid=(H, T // blk),
        in_specs=[pl.BlockSpec((None, blk, LANES), lambda h, i: (h, i, 0)),
                  pl.BlockSpec((None, T, LANES), lambda h, i: (H + h, 0, 0)),
                  pl.BlockSpec((None, T, LANES), lambda h, i: (2 * H + h, 0, 0)),
                  pl.BlockSpec((None, blk, blk), lambda h, i: (h, 0, 0)),
                  pl.BlockSpec((None, blk, blk), lambda h, i: (h, 0, 0)),
                  pl.BlockSpec((None, 4, DH_DIFF), lambda h, i: (l, 0, 0)),
                  pl.BlockSpec((None, 1, LANES), lambda h, i: (l, 0, 0))],
        out_specs=pl.BlockSpec((None, blk, LANES), lambda h, i: (h, i, 0)),
        out_shape=jax.ShapeDtypeStruct((H, T, LANES), BF16),
        scratch_shapes=[pltpu.VMEM((2 * blk, LANES), F32), pltpu.VMEM((2 * blk, LANES), F32),
                        pltpu.VMEM((2 * blk, LANES), F32)],
        compiler_params=_cparams(("arbitrary", "arbitrary")),
        name="diff_attn",
    )(zb, zb, zb, bias_diag, bias_sub, lam_p, g)


def _fox_attn_kernel(q_ref, k_ref, v_ref, ct_ref, o_ref, m_sc, l_sc, acc_sc, *, blk):
    h = pl.program_id(0)
    qi = pl.program_id(1)
    q = q_ref[...]
    qs = pl.multiple_of(qi * blk, blk)
    c_ref0 = ct_ref[pl.ds(h, 1), pl.ds(qs, blk)][:, 0:1]
    _flash_init(m_sc, l_sc, acc_sc)

    def step(kj, masked):
        ks = pl.multiple_of(kj * blk, blk)
        k = k_ref[pl.ds(ks, blk), :]
        v = v_ref[pl.ds(ks, blk), :]
        ck = (ct_ref[pl.ds(h, 1), pl.ds(ks, blk)] - c_ref0) * LOG2E
        s = lax.dot_general(q, k, NT_DIMS, preferred_element_type=F32) - ck
        if masked:
            row = lax.broadcasted_iota(jnp.int32, s.shape, 0)
            col = lax.broadcasted_iota(jnp.int32, s.shape, 1)
            s = jnp.where(row >= col, s, NEG)
        _flash_update(s, v, m_sc, l_sc, acc_sc)

    def far_body(kj, carry):
        step(kj, False)
        return carry

    lax.fori_loop(0, qi, far_body, 0)
    step(qi, True)
    o_ref[...] = (acc_sc[...] / _flash_rowsum(l_sc)).astype(o_ref.dtype)


def _fox_attention(zb, ct, blk):
    T = zb.shape[1]
    H = N_HEADS
    return pl.pallas_call(
        functools.partial(_fox_attn_kernel, blk=blk),
        grid=(H, T // blk),
        in_specs=[pl.BlockSpec((None, blk, LANES), lambda h, i: (3 * H + h, i, 0)),
                  pl.BlockSpec((None, T, LANES), lambda h, i: (4 * H + h, 0, 0)),
                  pl.BlockSpec((None, T, LANES), lambda h, i: (5 * H + h, 0, 0)),
                  pl.BlockSpec((N_HEADS, T), lambda h, i: (0, 0))],
        out_specs=pl.BlockSpec((None, blk, LANES), lambda h, i: (h, i, 0)),
        out_shape=jax.ShapeDtypeStruct((H, T, LANES), BF16),
        scratch_shapes=[pltpu.VMEM((blk, LANES), F32), pltpu.VMEM((blk, LANES), F32),
                        pltpu.VMEM((blk, LANES), F32)],
        compiler_params=_cparams(("arbitrary", "arbitrary")),
        name="fox_attn",
    )(zb, zb, zb, ct)


def _mm_pn_kernel(*refs, n_parts, alpha, tiled):
    a_refs = refs[:n_parts]
    w_ref, r_ref, g_ref, b_ref, of_ref, ob_ref = refs[n_parts:n_parts + 6]

    def finish(y):
        y = _post_norm(alpha * r_ref[...] + y, g_ref[...], b_ref[...])
        of_ref[...] = y
        ob_ref[...] = y.astype(BF16)

    if not tiled:
        y = None
        row = 0
        for a_ref in a_refs:
            a = a_ref[...]
            if a.ndim == 3:
                a = _lane_cat(a)
            d = jnp.dot(a, w_ref[row:row + a.shape[1], :], preferred_element_type=F32)
            y = d if y is None else y + d
            row += a.shape[1]
        finish(y)
        return

    acc = refs[n_parts + 6]
    k = pl.program_id(1)

    @pl.when(k == 0)
    def _():
        acc[...] = jnp.zeros_like(acc)

    acc[...] += jnp.dot(a_refs[0][...], w_ref[...], preferred_element_type=F32)

    @pl.when(k == pl.num_programs(1) - 1)
    def _():
        finish(acc[...])


def _mm_post_norm(parts, w3, l, resid, g3, b3, ln_idx, alpha, tm, tk=None):
    M, D = resid.shape
    tiled = tk is not None
    in_specs = []
    if tiled:
        assert len(parts) == 1 and parts[0].ndim == 2 and parts[0].shape[1] % tk == 0
        nk = parts[0].shape[1] // tk
        in_specs.append(pl.BlockSpec((tm, tk), lambda i, k: (i, k)))
        in_specs.append(pl.BlockSpec((None, tk, D), lambda i, k: (l, k, 0)))
    else:
        nk = 1
        for a in parts:
            if a.ndim == 3:
                in_specs.append(pl.BlockSpec((a.shape[0], tm, LANES), lambda i, k: (0, i, 0)))
            else:
                in_specs.append(pl.BlockSpec((tm, a.shape[1]), lambda i, k: (i, 0)))
        in_specs.append(pl.BlockSpec((None, w3.shape[1], D), lambda i, k: (l, 0, 0)))
    in_specs += [pl.BlockSpec((tm, D), lambda i, k: (i, 0)),
                 pl.BlockSpec((None, None, 1, D), lambda i, k: (l, ln_idx, 0, 0)),
                 pl.BlockSpec((None, None, 1, D), lambda i, k: (l, ln_idx, 0, 0))]
    return pl.pallas_call(
        functools.partial(_mm_pn_kernel, n_parts=len(parts), alpha=alpha, tiled=tiled),
        grid=(M // tm, nk),
        in_specs=in_specs,
        out_specs=[pl.BlockSpec((tm, D), lambda i, k: (i, 0)), pl.BlockSpec((tm, D), lambda i, k: (i, 0))],
        out_shape=[jax.ShapeDtypeStruct((M, D), F32), jax.ShapeDtypeStruct((M, D), BF16)],
        scratch_shapes=[pltpu.VMEM((tm, D), F32)] if tiled else [],
        compiler_params=_cparams(("arbitrary", "arbitrary")),
        name="mm_post_norm",
    )(*parts, w3, resid, g3, b3)


def _mem_attn_kernel(x_ref, wq_ref, mk_ref, mv_ref, wo_ref, r_ref, g_ref, b_ref, of_ref, ob_ref, *, alpha):
    scale = HEAD_W ** -0.5
    q = (jnp.dot(x_ref[...], wq_ref[...], preferred_element_type=F32) * scale).astype(BF16)
    mk = mk_ref[...]
    mv = mv_ref[...]
    outs = []
    for h in range(N_HEADS_MEM):
        sl = slice(h * HEAD_W, (h + 1) * HEAD_W)
        s = lax.dot_general(q[:, sl], mk[:, sl], NT_DIMS, preferred_element_type=F32)
        s = s - jnp.max(s, axis=-1, keepdims=True)
        p = jnp.exp(s)
        o = jnp.dot(p.astype(BF16), mv[:, sl], preferred_element_type=F32)
        outs.append((o / jnp.sum(p, axis=-1, keepdims=True)).astype(BF16))
    o = jnp.concatenate(outs, axis=1)
    y = jnp.dot(o, wo_ref[...], preferred_element_type=F32)
    y = _post_norm(alpha * r_ref[...] + y, g_ref[...], b_ref[...])
    of_ref[...] = y
    ob_ref[...] = y.astype(BF16)


def _mem_attention(xb, w_mq, mkb, mvb, w_mo, l, resid, g3, b3, alpha, tm):
    M, D = resid.shape
    W = w_mq.shape[2]
    NM = mkb.shape[0]
    return pl.pallas_call(
        functools.partial(_mem_attn_kernel, alpha=alpha),
        grid=(M // tm,),
        in_specs=[pl.BlockSpec((tm, D), lambda i: (i, 0)),
                  pl.BlockSpec((None, D, W), lambda i: (l, 0, 0)),
                  pl.BlockSpec((NM, W), lambda i: (0, 0)),
                  pl.BlockSpec((NM, W), lambda i: (0, 0)),
                  pl.BlockSpec((None, W, D), lambda i: (l, 0, 0)),
                  pl.BlockSpec((tm, D), lambda i: (i, 0)),
                  pl.BlockSpec((None, None, 1, D), lambda i: (l, 1, 0, 0)),
                  pl.BlockSpec((None, None, 1, D), lambda i: (l, 1, 0, 0))],
        out_specs=[pl.BlockSpec((tm, D), lambda i: (i, 0)), pl.BlockSpec((tm, D), lambda i: (i, 0))],
        out_shape=[jax.ShapeDtypeStruct((M, D), F32), jax.ShapeDtypeStruct((M, D), BF16)],
        compiler_params=_cparams(("arbitrary",)),
        name="mem_attn",
    )(xb, w_mq, mkb, mvb, w_mo, resid, g3, b3)


def _swiglu_kernel(x_ref, wg_ref, wu_ref, o_ref):
    x = x_ref[...]
    g = jnp.dot(x, wg_ref[...], preferred_element_type=F32)
    u = jnp.dot(x, wu_ref[...], preferred_element_type=F32)
    o_ref[...] = (g * jax.nn.sigmoid(g) * u).astype(o_ref.dtype)


def _swiglu_up(xb, w_ff_in, l, tm, tn):
    M, D = xb.shape
    dff = w_ff_in.shape[2] // 2
    nj = dff // tn
    return pl.pallas_call(
        _swiglu_kernel,
        grid=(nj, M // tm),
        in_specs=[pl.BlockSpec((tm, D), lambda j, i: (i, 0)),
                  pl.BlockSpec((None, D, tn), lambda j, i: (l, 0, j)),
                  pl.BlockSpec((None, D, tn), lambda j, i: (l, 0, j + nj))],
        out_specs=pl.BlockSpec((tm, tn), lambda j, i: (i, j)),
        out_shape=jax.ShapeDtypeStruct((M, dff), BF16),
        compiler_params=_cparams(("arbitrary", "arbitrary")),
        name="swiglu_up",
    )(xb, w_ff_in, w_ff_in)


def _lane_suffix_tile(x):
    row = lax.broadcasted_iota(jnp.int32, x.shape, 0)
    lane = lax.broadcasted_iota(jnp.int32, x.shape, 1)
    y = x
    z = x
    for sh in (8, 16, 32, 64):
        shifted = pltpu.roll(y, LANES - sh, 1)
        y = y + jnp.where(lane + sh < LANES, shifted, 0.0)
        z = z + pltpu.roll(z, sh, 1)
    a = z
    for sh in (1, 2, 4):
        shifted = pltpu.roll(a, 8 - sh, 0)
        a = a + jnp.where(row + sh < 8, shifted, 0.0)
    return (y - x) + (a - z), a[0:1, :]


def _decode_kernel(pt_ref, qd_ref, qf_ref, kdn_ref, vdn_ref, kfn_ref, vfn_ref, lfn_ref,
                   blast_ref, bnew_ref, lam_ref, g_ref, *refs, n_pages, lam0):
    P = n_pages
    kd, vd, kf, vf, lfp = (refs[i * P:(i + 1) * P] for i in range(5))
    od_ref, of_ref = refs[5 * P:5 * P + 2]
    qd_sc, qf_sc, md, ld, accd, mf, lf_sum, accf, carry = refs[5 * P + 2:]
    c = pl.program_id(1)
    H = N_HEADS
    PS = kd[0].shape[0]
    R = PS * H

    @pl.when(c == 0)
    def _():
        q = qd_ref[...].astype(F32)
        lane = lax.broadcasted_iota(jnp.int32, q.shape, 1)
        qq = jnp.concatenate([jnp.where(lane < DH_DIFF, q, 0.0), jnp.where(lane >= DH_DIFF, q, 0.0)], axis=0)
        qd_sc[...] = qq.astype(BF16)
        qf_sc[...] = qf_ref[...]
        kn = kdn_ref[...].astype(BF16).astype(F32)
        m0 = jnp.sum(qq * jnp.concatenate([kn, kn], axis=0), axis=-1, keepdims=True) + bnew_ref[...][:, 0:1]
        md[...] = jnp.broadcast_to(m0, md.shape)
        ld[...] = jnp.full_like(ld, 1.0 / LANES)
        vn = vdn_ref[...].astype(BF16).astype(F32)
        accd[...] = jnp.concatenate([vn, vn], axis=0)
        m0 = jnp.sum(qf_ref[...].astype(F32) * kfn_ref[...].astype(BF16).astype(F32), axis=-1, keepdims=True)
        mf[...] = jnp.broadcast_to(m0, mf.shape)
        lf_sum[...] = jnp.full_like(lf_sum, 1.0 / LANES)
        accf[...] = vfn_ref[...].astype(BF16).astype(F32)
        lane1 = lax.broadcasted_iota(jnp.int32, (1, LANES), 1)
        v = jnp.where(lane1 < H, lfn_ref[...], 0.0)
        for sh in (8, 16, 32, 64):
            v = v + pltpu.roll(v, sh, 1)
        carry[...] = v

    def own_head(rows):
        r = lax.broadcasted_iota(jnp.int32, (rows, R), 0)
        col = lax.broadcasted_iota(jnp.int32, (rows, R), 1)
        return jnp.bitwise_and(col, H - 1) == jnp.bitwise_and(r, H - 1)

    own_d = own_head(2 * H)
    own_f = own_head(H)
    is_last = (c == 0).astype(F32)

    dec = [None] * P
    cv = carry[...]
    for r in range(P - 1, -1, -1):
        inner, tot = _lane_suffix_tile(lfp[r][...])
        tile = (inner + cv) * LOG2E
        dec[r] = jnp.concatenate([tile[i:i + 1, :] for i in range(8)], axis=1)
        cv = cv + tot
    carry[...] = cv

    sd, sf = [], []
    for r in range(P):
        s = lax.dot_general(qd_sc[...], kd[r][...].reshape(R, HEAD_W).astype(BF16), NT_DIMS,
                            preferred_element_type=F32)
        if r == P - 1:
            s = s + is_last * blast_ref[...]
        sd.append(jnp.where(own_d, s, NEG))
        s = lax.dot_general(qf_sc[...], kf[r][...].reshape(R, HEAD_W).astype(BF16), NT_DIMS,
                            preferred_element_type=F32) + dec[r]
        sf.append(jnp.where(own_f, s, NEG))

    def update(s_pages, v_refs, m_sc, l_sc, acc_sc):
        m_prev = m_sc[...]
        m_cur = s_pages[0]
        for s in s_pages[1:]:
            m_cur = jnp.maximum(m_cur, s)
        m_new = jnp.maximum(m_prev, jnp.max(m_cur, axis=-1, keepdims=True))
        alpha = jnp.exp2(m_prev - m_new)
        m_sc[...] = m_new
        m_wide = jnp.concatenate([m_new] * (R // LANES), axis=1)
        acc = alpha * acc_sc[...]
        lsum = alpha * l_sc[...]
        for s, v_ref in zip(s_pages, v_refs):
            p = jnp.exp2(s - m_wide)
            for i in range(R // LANES):
                lsum = lsum + p[:, i * LANES:(i + 1) * LANES]
            acc = acc + jnp.dot(p.astype(BF16), v_ref[...].reshape(R, HEAD_W).astype(BF16),
                                preferred_element_type=F32)
        l_sc[...] = lsum
        acc_sc[...] = acc

    update(sd, vd, md, ld, accd)
    update(sf, vf, mf, lf_sum, accf)

    @pl.when(c == pl.num_programs(1) - 1)
    def _():
        lam = _lam_value(lam_ref[...], lam0)
        o = accd[...] / jnp.sum(ld[...], axis=-1, keepdims=True)
        o = o[:H] - lam * o[H:]
        o = o * lax.rsqrt(jnp.mean(o * o, axis=-1, keepdims=True) + LN_EPS)
        od_ref[...] = (o * g_ref[...] * (1.0 - lam0)).astype(od_ref.dtype)
        of_ref[...] = (accf[...] / jnp.sum(lf_sum[...], axis=-1, keepdims=True)).astype(of_ref.dtype)


def _decode_attention(page_table, zsb, new_kv, lf_new, caches, bias_last, bias_new, lam_p, g, l, lam0, n_pages):
    B = zsb.shape[0]
    H = N_HEADS
    ckd, cvd, ckf, cvf, clf = caches
    PS = ckd.shape[2]
    per_seq = page_table.shape[1]
    nch = per_seq // n_pages

    def group_spec(g_idx):
        return pl.BlockSpec((None, H, HEAD_W), lambda b, c, pt: (b, g_idx, 0))

    new_spec = pl.BlockSpec((None, None, H, HEAD_W), lambda b, c, pt: (l, b, 0, 0))
    in_specs = [group_spec(0), group_spec(3), new_spec, new_spec, new_spec, new_spec,
                pl.BlockSpec((None, 1, LANES), lambda b, c, pt: (b, 0, 0)),
                pl.BlockSpec((1, PS * H), lambda b, c, pt: (0, 0)),
                pl.BlockSpec((2 * H, LANES), lambda b, c, pt: (0, 0)),
                pl.BlockSpec((None, 4, DH_DIFF), lambda b, c, pt: (l, 0, 0)),
                pl.BlockSpec((None, 1, HEAD_W), lambda b, c, pt: (l, 0, 0))]
    args = [zsb, zsb, *new_kv, lf_new.reshape(B, 1, LANES), bias_last, bias_new, lam_p, g]
    for cache in (ckd, cvd, ckf, cvf):
        for r in range(n_pages):
            in_specs.append(pl.BlockSpec((None, None, PS, H, HEAD_W),
                                         lambda b, c, pt, r=r: (l, pt[b, (nch - 1 - c) * n_pages + r], 0, 0, 0)))
            args.append(cache)
    for r in range(n_pages):
        in_specs.append(pl.BlockSpec((None, None, 8, LANES),
                                     lambda b, c, pt, r=r: (l, pt[b, (nch - 1 - c) * n_pages + r], 0, 0)))
        args.append(clf)
    out_spec = pl.BlockSpec((None, H, HEAD_W), lambda b, c, pt: (b, 0, 0))
    grid_spec = pltpu.PrefetchScalarGridSpec(
        num_scalar_prefetch=1, grid=(B, nch), in_specs=in_specs, out_specs=[out_spec, out_spec],
        scratch_shapes=[pltpu.VMEM((2 * H, HEAD_W), BF16), pltpu.VMEM((H, HEAD_W), BF16),
                        pltpu.VMEM((2 * H, LANES), F32), pltpu.VMEM((2 * H, LANES), F32),
                        pltpu.VMEM((2 * H, HEAD_W), F32),
                        pltpu.VMEM((H, LANES), F32), pltpu.VMEM((H, LANES), F32), pltpu.VMEM((H, HEAD_W), F32),
                        pltpu.VMEM((1, LANES), F32)])
    return pl.pallas_call(
        functools.partial(_decode_kernel, n_pages=n_pages, lam0=lam0),
        grid_spec=grid_spec,
        out_shape=[jax.ShapeDtypeStruct((B, H, HEAD_W), BF16), jax.ShapeDtypeStruct((B, H, HEAD_W), BF16)],
        compiler_params=_cparams(("arbitrary", "arbitrary")),
        name="decode_attn",
    )(page_table, *args)


def _mem_decode_kernel(q_ref, mk_ref, mv_ref, o_ref):
    q = q_ref[...].astype(F32)
    s = jnp.sum(mk_ref[...] * q[None], axis=-1, keepdims=True)
    p = jnp.exp(s - jnp.max(s, axis=0, keepdims=True))
    o = jnp.sum(p * mv_ref[...], axis=0) / jnp.sum(p, axis=0)
    o_ref[...] = o.astype(o_ref.dtype)


def _mem_decode_attention(qb, cmk, cmv, l):
    B, W = qb.shape
    NM, HM = cmk.shape[2:4]
    out = pl.pallas_call(
        _mem_decode_kernel,
        grid=(B,),
        in_specs=[pl.BlockSpec((None, HM, HEAD_W), lambda b: (b, 0, 0)),
                  pl.BlockSpec((None, None, NM, HM, HEAD_W), lambda b: (l, b, 0, 0, 0)),
                  pl.BlockSpec((None, None, NM, HM, HEAD_W), lambda b: (l, b, 0, 0, 0))],
        out_specs=pl.BlockSpec((None, HM, HEAD_W), lambda b: (b, 0, 0)),
        out_shape=jax.ShapeDtypeStruct((B, HM, HEAD_W), BF16),
        compiler_params=_cparams(("arbitrary",)),
        name="mem_decode_attn",
    )(qb.reshape(B, HM, HEAD_W), cmk, cmv)
    return out.reshape(B, W)


def _bias_by_distance(rel_bias, dist):
    bucket = _rel_bucket(dist)
    rel = (rel_bias - rel_bias[N_BUCKETS - 1]) * LOG2E
    out = jnp.zeros((rel_bias.shape[1],) + dist.shape, F32)
    for b in range(N_BUCKETS - 1):
        out = jnp.where((bucket == b)[None], rel[b].reshape((-1,) + (1,) * dist.ndim), out)
    return out


def _bias_tables(rel_bias, blk):
    assert blk >= MAX_DISTANCE
    r = jnp.arange(blk, dtype=jnp.int32)[:, None]
    c = jnp.arange(blk, dtype=jnp.int32)[None, :]
    diag = jnp.where((r >= c)[None], _bias_by_distance(rel_bias, jnp.maximum(r - c, 0)), NEG)
    sub = _bias_by_distance(rel_bias, blk + r - c)
    return diag, sub


def _decode_bias(rel_bias, page_size):
    assert page_size >= MAX_DISTANCE
    last = _bias_by_distance(rel_bias, page_size - jnp.arange(page_size, dtype=jnp.int32))
    new = jnp.broadcast_to(_bias_by_distance(rel_bias, jnp.zeros((1,), jnp.int32)), (N_HEADS, LANES))
    return last.T.reshape(1, page_size * N_HEADS), jnp.concatenate([new, new], axis=0)


def kernel(x_prompt, x_sample, cache_diff_k, cache_diff_v, cache_fox_k, cache_fox_v, cache_fox_logf,
           cache_mem_k, cache_mem_v, page_table, mem_prompt, w_in, b_forget, diff_lambda, diff_subln_g,
           rel_bias, w_o, w_mq, w_mkv, w_mo, w_ff_in, w_ff_out, ln_g, ln_b):
    depth = w_in.shape[0]
    alpha = (2 * depth) ** 0.25
    _, T, D = x_prompt.shape
    DB = x_sample.shape[0]
    W = N_HEADS * HEAD_W
    n_qkv = 6 * W
    pool, page_size = cache_diff_k.shape[1:3]
    n_mem = mem_prompt.shape[1]
    w_mem = w_mq.shape[2]
    blk = min(ATT_BLK, T)
    blk_fox = 2 * blk if T % (2 * blk) == 0 else blk
    tm = min(512, T)
    assert rel_bias.shape == (N_BUCKETS, N_HEADS) and w_in.shape[2] == n_qkv + N_HEADS

    w_in_b = w_in.astype(BF16)
    w_fl_b = jnp.pad(w_in[:, :, n_qkv:], ((0, 0), (0, 0), (0, LANES - N_HEADS))).astype(BF16)
    b_fl = jnp.pad(b_forget, ((0, 0), (0, LANES - N_HEADS))).reshape(depth, 1, LANES)
    w_o_b, w_mq_b, w_mkv_b, w_mo_b = (w.astype(BF16) for w in (w_o, w_mq, w_mkv, w_mo))
    w_ff_in_b, w_ff_out_b = w_ff_in.astype(BF16), w_ff_out.astype(BF16)
    ln_g4 = ln_g.reshape(depth, 3, 1, D)
    ln_b4 = ln_b.reshape(depth, 3, 1, D)
    g_sub = diff_subln_g.reshape(depth, 1, HEAD_W)
    qk_scales = (DH_DIFF ** -0.5 * LOG2E, 1.0, 1.0, HEAD_W ** -0.5 * LOG2E, 1.0, 1.0)
    bias_diag, bias_sub = _bias_tables(rel_bias, blk)
    bias_last, bias_new = _decode_bias(rel_bias, page_size)
    d_ff = w_ff_out.shape[1]
    tn_ff = 512 if d_ff % 512 == 0 else d_ff
    tk_ff = d_ff // 4 if d_ff % (4 * LANES) == 0 else d_ff

    def ffn(xf, xb, l, tmm, tm_up):
        hff = _swiglu_up(xb, w_ff_in_b, l, tm_up, tn_ff)
        return _mm_post_norm([hff], w_ff_out_b, l, xf, ln_g4, ln_b4, 2, alpha, tmm, tk_ff)

    xf = x_prompt.reshape(T, D)
    xb = xf.astype(BF16)
    memb = mem_prompt.reshape(n_mem, D).astype(BF16)
    p_kv, p_lf, p_mk, p_mv = None, [], [], []
    for l in range(depth):
        lam0 = _lambda_init(l)
        *p_kv, zb = _in_proj(xb, w_in_b, l, tm, qk_scales, p_kv)
        lf, ct = _logits(xb, w_fl_b, b_fl, l, tm, True)
        alt = l == 1
        od = _diff_attention(zb, bias_diag, bias_sub, diff_lambda, g_sub, l, lam0, blk, bf16_exp=alt)
        of = _fox_attention(zb, ct, blk_fox)
        xf, xb = _mm_post_norm([od, of], w_o_b, l, xf, ln_g4, ln_b4, 0, alpha, tm)
        mkv_f, mkv_b = _proj(memb, w_mkv_b, l, w_mem, n_mem, want_f32=True)
        xf, xb = _mem_attention(xb, w_mq_b, mkv_b[:, :w_mem], mkv_b[:, w_mem:], w_mo_b, l, xf, ln_g4, ln_b4, alpha, tm)
        xf, xb = ffn(xf, xb, l, tm, min(T, 4 * tm if alt else 2 * tm))
        p_lf.append(lf[:, :N_HEADS])
        p_mk.append(mkv_f[:, :w_mem])
        p_mv.append(mkv_f[:, w_mem:])
    y_prompt = xf.reshape(1, T, D)

    assert page_size * N_HEADS == 8 * LANES
    caches = (cache_diff_k, cache_diff_v, cache_fox_k, cache_fox_v, cache_fox_logf.reshape(depth, pool, 8, LANES))
    per_seq = page_table.shape[1]
    n_pages = 8 if per_seq % 8 == 0 else per_seq
    xf = x_sample.reshape(DB, D)
    xb = xf.astype(BF16)
    s_kv, s_lf = None, []
    for l in range(depth):
        lam0 = _lambda_init(l)
        *s_kv, zb = _in_proj(xb, w_in_b, l, DB, qk_scales, s_kv)
        (lf,) = _logits(xb, w_fl_b, b_fl, l, DB, False)
        od, of = _decode_attention(page_table, jnp.swapaxes(zb, 0, 1), s_kv, lf, caches,
                                   bias_last, bias_new, diff_lambda, g_sub, l, lam0, n_pages)
        xf, xb = _mm_post_norm([od.reshape(DB, W), of.reshape(DB, W)], w_o_b, l, xf, ln_g4, ln_b4, 0, alpha, DB)
        (qm,) = _proj(xb, w_mq_b, l, w_mem, DB, scale=HEAD_W ** -0.5)
        om = _mem_decode_attention(qm, cache_mem_k, cache_mem_v, l)
        xf, xb = _mm_post_norm([om], w_mo_b, l, xf, ln_g4, ln_b4, 1, alpha, DB)
        xf, xb = ffn(xf, xb, l, DB, DB)
        s_lf.append(lf[:, :N_HEADS])
    y_sample = xf.reshape(DB, 1, D)

    def st(parts, *shape):
        return jnp.stack(parts).reshape(depth, *shape)

    return (y_prompt, y_sample,
            *(a.reshape(depth, 1, T, N_HEADS, HEAD_W) for a in p_kv), st(p_lf, 1, T, N_HEADS),
            st(p_mk, 1, n_mem, N_HEADS_MEM, HEAD_W), st(p_mv, 1, n_mem, N_HEADS_MEM, HEAD_W),
            *(a.reshape(depth, DB, 1, N_HEADS, HEAD_W) for a in s_kv), st(s_lf, DB, 1, N_HEADS))
```

```python
import functools
import math

import jax
import jax.numpy as jnp
from jax import lax
from jax.experimental import pallas as pl
from jax.experimental.pallas import tpu as pltpu

F32 = jnp.float32
BF16 = jnp.bfloat16

LANES = 128
LOG2E = 1.4426950408889634
NEG = -1e30
LN_EPS = 1e-5
N_BUCKETS = 32
MAX_DISTANCE = 128
DH_DIFF = 64
HEAD_W = 128
N_HEADS = 8
N_HEADS_MEM = 4
ATT_BLK = 512
VMEM_LIMIT = 56 * 1024 * 1024

NT_DIMS = (((1,), (1,)), ((), ()))


def _cparams(sem):
    return pltpu.CompilerParams(dimension_semantics=sem, vmem_limit_bytes=VMEM_LIMIT)


def _lambda_init(layer):
    return 0.8 - 0.6 * math.exp(-0.3 * layer)


def _rel_bucket(dist):
    n = jnp.maximum(dist, 0)
    max_exact = N_BUCKETS // 2
    nf = jnp.maximum(n, 1).astype(F32)
    large = max_exact + (jnp.log(nf / max_exact) / math.log(MAX_DISTANCE / max_exact)
                         * (N_BUCKETS - max_exact)).astype(jnp.int32)
    large = jnp.minimum(large, N_BUCKETS - 1)
    return jnp.where(n < max_exact, n, large)


def _split3(x):
    hi = x.astype(BF16)
    r1 = x - hi.astype(F32)
    mid = r1.astype(BF16)
    lo = (r1 - mid.astype(F32)).astype(BF16)
    return hi, mid, lo


def _lane_cat(a):
    return jnp.concatenate([a[h] for h in range(a.shape[0])], axis=1)


def _post_norm(h, g, b):
    mu = jnp.mean(h, axis=-1, keepdims=True)
    d = h - mu
    var = jnp.mean(d * d, axis=-1, keepdims=True)
    return d * lax.rsqrt(var + LN_EPS) * g + b


def _lam_value(lp, lam0):
    a = jnp.sum(lp[0:1] * lp[1:2], axis=-1, keepdims=True)
    b = jnp.sum(lp[2:3] * lp[3:4], axis=-1, keepdims=True)
    return jnp.exp(a) - jnp.exp(b) + lam0


def _proj_kernel(x_ref, w_ref, *out_refs, scale, want_f32):
    acc = jnp.dot(x_ref[...], w_ref[...], preferred_element_type=F32)
    if want_f32:
        out_refs[0][...] = acc
    out_refs[-1][...] = (acc * scale).astype(BF16)


def _proj(x, w3, l, tn, tm, scale=1.0, want_f32=False):
    M, K = x.shape
    N = w3.shape[2]
    spec = pl.BlockSpec((tm, tn), lambda j, i: (i, j))
    dtypes = ([F32] if want_f32 else []) + [BF16]
    return pl.pallas_call(
        functools.partial(_proj_kernel, scale=scale, want_f32=want_f32),
        grid=(N // tn, M // tm),
        in_specs=[pl.BlockSpec((tm, K), lambda j, i: (i, 0)),
                  pl.BlockSpec((None, K, tn), lambda j, i: (l, 0, j))],
        out_specs=[spec] * len(dtypes), out_shape=[jax.ShapeDtypeStruct((M, N), d) for d in dtypes],
        compiler_params=_cparams(("arbitrary", "arbitrary")),
        name="proj",
    )(x, w3)


F32_GROUPS = (1, 2, 4, 5)


def _in_proj_kernel(x_ref, w_ref, *refs, scales):
    dk_ref, dv_ref, fk_ref, fv_ref, zb_ref = refs[-5:]
    f32_out = dict(zip(F32_GROUPS, (dk_ref, dv_ref, fk_ref, fv_ref)))
    j = pl.program_id(1)

    def group(g):
        acc = jnp.dot(x_ref[...], w_ref[...], preferred_element_type=F32)
        tm = acc.shape[0]
        if g in f32_out:
            f32_out[g][...] = acc.reshape(tm, N_HEADS, HEAD_W)
        y = (acc * scales[g]).astype(BF16) if scales[g] != 1.0 else acc.astype(BF16)
        for hh in range(N_HEADS):
            zb_ref[hh] = y[:, hh * HEAD_W:(hh + 1) * HEAD_W]

    for g in range(len(scales)):
        pl.when(j == g)(functools.partial(group, g))


def _in_proj(x, w3, l, tm, scales, prev=None):
    M, K = x.shape
    depth = w3.shape[0]
    W = N_HEADS * HEAD_W
    f32_spec = pl.BlockSpec((None, tm, N_HEADS, HEAD_W), lambda i, j: (l, i, 0, 0))
    f32_shape = jax.ShapeDtypeStruct((depth, M, N_HEADS, HEAD_W), F32)
    prev = list(prev) if prev is not None else []
    return pl.pallas_call(
        functools.partial(_in_proj_kernel, scales=tuple(scales)),
        grid=(M // tm, len(scales)),
        in_specs=[pl.BlockSpec((tm, K), lambda i, j: (i, 0)),
                  pl.BlockSpec((None, K, W), lambda i, j: (l, 0, j))]
                 + [pl.BlockSpec(memory_space=pl.ANY)] * len(prev),
        out_specs=[f32_spec] * 4 + [pl.BlockSpec((N_HEADS, tm, HEAD_W), lambda i, j: (j, i, 0))],
        out_shape=[f32_shape] * 4 + [jax.ShapeDtypeStruct((len(scales) * N_HEADS, M, HEAD_W), BF16)],
        input_output_aliases={2 + k: k for k in range(len(prev))},
        compiler_params=_cparams(("arbitrary", "arbitrary")),
        name="in_proj",
    )(x, w3, *prev)


def _logit_kernel(x_ref, w_ref, b_ref, lf_ref, *rest, n_valid, with_cumsum):
    fl = jnp.dot(x_ref[...], w_ref[...], preferred_element_type=F32) + b_ref[...]
    lf = jnp.minimum(fl, 0.0) - jnp.log1p(jnp.exp(-jnp.abs(fl)))
    tm = lf.shape[0]
    lane = lax.broadcasted_iota(jnp.int32, lf.shape, 1)
    lf = jnp.where(lane < n_valid, lf, 0.0)
    lf_ref[...] = lf
    if with_cumsum:
        ct_ref, carry = rest

        @pl.when(pl.program_id(0) == 0)
        def _():
            carry[...] = jnp.zeros_like(carry)

        row = lax.broadcasted_iota(jnp.int32, (tm, tm), 0)
        col = lax.broadcasted_iota(jnp.int32, (tm, tm), 1)
        tri = (row >= col).astype(BF16)
        hi, mid, lo = _split3(lf)
        c = (jnp.dot(tri, hi, preferred_element_type=F32)
             + jnp.dot(tri, mid, preferred_element_type=F32)
             + jnp.dot(tri, lo, preferred_element_type=F32)) + carry[...]
        ct_ref[...] = c.T[:ct_ref.shape[0], :]
        carry[...] = c[tm - 1:tm, :]


def _logits(x, wl3, bl3, l, tm, with_cumsum):
    M, K = x.shape
    out_shape = [jax.ShapeDtypeStruct((M, LANES), F32)]
    out_specs = [pl.BlockSpec((tm, LANES), lambda i: (i, 0))]
    scratch = []
    if with_cumsum:
        out_shape += [jax.ShapeDtypeStruct((N_HEADS, M), F32)]
        out_specs += [pl.BlockSpec((N_HEADS, tm), lambda i: (0, i))]
        scratch = [pltpu.VMEM((1, LANES), F32)]
    return pl.pallas_call(
        functools.partial(_logit_kernel, n_valid=N_HEADS, with_cumsum=with_cumsum),
        grid=(M // tm,),
        in_specs=[pl.BlockSpec((tm, K), lambda i: (i, 0)),
                  pl.BlockSpec((None, K, LANES), lambda i: (l, 0, 0)),
                  pl.BlockSpec((None, 1, LANES), lambda i: (l, 0, 0))],
        out_specs=out_specs, out_shape=out_shape, scratch_shapes=scratch,
        compiler_params=_cparams(("arbitrary",)),
        name="logits",
    )(x, wl3, bl3)


def _flash_init(m_sc, l_sc, acc_sc):
    m_sc[...] = jnp.full_like(m_sc, NEG)
    l_sc[...] = jnp.zeros_like(l_sc)
    acc_sc[...] = jnp.zeros_like(acc_sc)


def _flash_update(s, v, m_sc, l_sc, acc_sc, bf16_exp=False):
    n = s.shape[1] // LANES
    m_prev = m_sc[...]
    m_new = jnp.maximum(m_prev, jnp.max(s, axis=-1, keepdims=True))
    alpha = jnp.exp2(m_prev - m_new)
    z = s - jnp.concatenate([m_new] * n, axis=1)
    p = jnp.exp2(z.astype(BF16)) if bf16_exp else jnp.exp2(z)
    psum = p[:, :LANES]
    for c in range(1, n):
        psum = psum + p[:, c * LANES:(c + 1) * LANES]
    l_sc[...] = alpha * l_sc[...] + psum.astype(F32)
    acc_sc[...] = alpha * acc_sc[...] + jnp.dot(p.astype(BF16), v, preferred_element_type=F32)
    m_sc[...] = m_new


def _flash_rowsum(l_sc):
    return jnp.sum(l_sc[...], axis=-1, keepdims=True)


def _diff_attn_kernel(q_ref, k_ref, v_ref, bd_ref, bs_ref, lam_ref, g_ref, o_ref,
                      m_sc, l_sc, acc_sc, *, blk, lam0, bf16_exp):
    qi = pl.program_id(1)
    q = q_ref[...].astype(F32)
    lane = lax.broadcasted_iota(jnp.int32, q.shape, 1)
    qq = jnp.concatenate([jnp.where(lane < DH_DIFF, q, 0.0), jnp.where(lane >= DH_DIFF, q, 0.0)], axis=0).astype(BF16)
    _flash_init(m_sc, l_sc, acc_sc)

    def step(kj, bias_ref):
        ks = pl.multiple_of(kj * blk, blk)
        k = k_ref[pl.ds(ks, blk), :]
        v = v_ref[pl.ds(ks, blk), :]
        s = lax.dot_general(qq, k, NT_DIMS, preferred_element_type=F32)
        if bias_ref is not None:
            b = bias_ref[...]
            s = s + jnp.concatenate([b, b], axis=0)
        _flash_update(s, v, m_sc, l_sc, acc_sc, bf16_exp)

    def far_body(kj, carry):
        step(kj, None)
        return carry

    lax.fori_loop(0, jnp.maximum(qi - 1, 0), far_body, 0)

    @pl.when(qi >= 1)
    def _():
        step(qi - 1, bs_ref)

    step(qi, bd_ref)

    lam = _lam_value(lam_ref[...], lam0)
    o = acc_sc[...] / _flash_rowsum(l_sc)
    o = o[:blk] - lam * o[blk:]
    o = o * lax.rsqrt(jnp.mean(o * o, axis=-1, keepdims=True) + LN_EPS)
    o_ref[...] = (o * g_ref[...] * (1.0 - lam0)).astype(o_ref.dtype)


def _diff_attention(zb, bias_diag, bias_sub, lam_p, g, l, lam0, blk, bf16_exp=False):
    T = zb.shape[1]
    H = N_HEADS
    return pl.pallas_call(
        functools.partial(_diff_attn_kernel, blk=blk, lam0=lam0, bf16_exp=bf16_exp),
        grid=(H, T // blk),
        in_specs=[pl.BlockSpec((None, blk, LANES), lambda h, i: (h, i, 0)),
                  pl.BlockSpec((None, T, LANES), lambda h, i: (H + h, 0, 0)),
                  pl.BlockSpec((None, T, LANES), lambda h, i: (2 * H + h, 0, 0)),
                  pl.BlockSpec((None, blk, blk), lambda h, i: (h, 0, 0)),
                  pl.BlockSpec((None, blk, blk), lambda h, i: (h, 0, 0)),
                  pl.BlockSpec((None, 4, DH_DIFF), lambda h, i: (l, 0, 0)),
                  pl.BlockSpec((None, 1, LANES), lambda h, i: (l, 0, 0))],
        out_specs=pl.BlockSpec((None, blk, LANES), lambda h, i: (h, i, 0)),
        out_shape=jax.ShapeDtypeStruct((H, T, LANES), BF16),
        scratch_shapes=[pltpu.VMEM((2 * blk, LANES), F32), pltpu.VMEM((2 * blk, LANES), F32),
                        pltpu.VMEM((2 * blk, LANES), F32)],
        compiler_params=_cparams(("arbitrary", "arbitrary")),
        name="diff_attn",
    )(zb, zb, zb, bias_diag, bias_sub, lam_p, g)


def _fox_attn_kernel(q_ref, k_ref, v_ref, ct_ref, o_ref, m_sc, l_sc, acc_sc, *, blk, bf16_exp):
    h = pl.program_id(0)
    qi = pl.program_id(1)
    q = q_ref[...]
    qs = pl.multiple_of(qi * blk, blk)
    c_ref0 = ct_ref[pl.ds(h, 1), pl.ds(qs, blk)][:, 0:1]
    _flash_init(m_sc, l_sc, acc_sc)

    def step(kj, masked):
        ks = pl.multiple_of(kj * blk, blk)
        k = k_ref[pl.ds(ks, blk), :]
        v = v_ref[pl.ds(ks, blk), :]
        ck = (ct_ref[pl.ds(h, 1), pl.ds(ks, blk)] - c_ref0) * LOG2E
        s = lax.dot_general(q, k, NT_DIMS, preferred_element_type=F32) - ck
        if masked:
            row = lax.broadcasted_iota(jnp.int32, s.shape, 0)
            col = lax.broadcasted_iota(jnp.int32, s.shape, 1)
            s = jnp.where(row >= col, s, NEG)
        _flash_update(s, v, m_sc, l_sc, acc_sc, bf16_exp)

    def far_body(kj, carry):
        step(kj, False)
        return carry

    lax.fori_loop(0, qi, far_body, 0)
    step(qi, True)
    o_ref[...] = (acc_sc[...] / _flash_rowsum(l_sc)).astype(o_ref.dtype)


def _fox_attention(zb, ct, blk, bf16_exp=False):
    T = zb.shape[1]
    H = N_HEADS
    return pl.pallas_call(
        functools.partial(_fox_attn_kernel, blk=blk, bf16_exp=bf16_exp),
        grid=(H, T // blk),
        in_specs=[pl.BlockSpec((None, blk, LANES), lambda h, i: (3 * H + h, i, 0)),
                  pl.BlockSpec((None, T, LANES), lambda h, i: (4 * H + h, 0, 0)),
                  pl.BlockSpec((None, T, LANES), lambda h, i: (5 * H + h, 0, 0)),
                  pl.BlockSpec((N_HEADS, T), lambda h, i: (0, 0))],
        out_specs=pl.BlockSpec((None, blk, LANES), lambda h, i: (h, i, 0)),
        out_shape=jax.ShapeDtypeStruct((H, T, LANES), BF16),
        scratch_shapes=[pltpu.VMEM((blk, LANES), F32), pltpu.VMEM((blk, LANES), F32),
                        pltpu.VMEM((blk, LANES), F32)],
        compiler_params=_cparams(("arbitrary", "arbitrary")),
        name="fox_attn",
    )(zb, zb, zb, ct)


def _mm_pn_kernel(*refs, n_parts, alpha, tiled):
    a_refs = refs[:n_parts]
    w_ref, r_ref, g_ref, b_ref, of_ref, ob_ref = refs[n_parts:n_parts + 6]

    def finish(y):
        y = _post_norm(alpha * r_ref[...] + y, g_ref[...], b_ref[...])
        of_ref[...] = y
        ob_ref[...] = y.astype(BF16)

    if not tiled:
        y = None
        row = 0
        for a_ref in a_refs:
            a = a_ref[...]
            if a.ndim == 3:
                a = _lane_cat(a)
            d = jnp.dot(a, w_ref[row:row + a.shape[1], :], preferred_element_type=F32)
            y = d if y is None else y + d
            row += a.shape[1]
        finish(y)
        return

    acc = refs[n_parts + 6]
    k = pl.program_id(1)

    @pl.when(k == 0)
    def _():
        acc[...] = jnp.zeros_like(acc)

    acc[...] += jnp.dot(a_refs[0][...], w_ref[...], preferred_element_type=F32)

    @pl.when(k == pl.num_programs(1) - 1)
    def _():
        finish(acc[...])


def _mm_post_norm(parts, w3, l, resid, g3, b3, ln_idx, alpha, tm, tk=None):
    M, D = resid.shape
    tiled = tk is not None
    in_specs = []
    if tiled:
        assert len(parts) == 1 and parts[0].ndim == 2 and parts[0].shape[1] % tk == 0
        nk = parts[0].shape[1] // tk
        in_specs.append(pl.BlockSpec((tm, tk), lambda i, k: (i, k)))
        in_specs.append(pl.BlockSpec((None, tk, D), lambda i, k: (l, k, 0)))
    else:
        nk = 1
        for a in parts:
            if a.ndim == 3:
                in_specs.append(pl.BlockSpec((a.shape[0], tm, LANES), lambda i, k: (0, i, 0)))
            else:
                in_specs.append(pl.BlockSpec((tm, a.shape[1]), lambda i, k: (i, 0)))
        in_specs.append(pl.BlockSpec((None, w3.shape[1], D), lambda i, k: (l, 0, 0)))
    in_specs += [pl.BlockSpec((tm, D), lambda i, k: (i, 0)),
                 pl.BlockSpec((None, None, 1, D), lambda i, k: (l, ln_idx, 0, 0)),
                 pl.BlockSpec((None, None, 1, D), lambda i, k: (l, ln_idx, 0, 0))]
    return pl.pallas_call(
        functools.partial(_mm_pn_kernel, n_parts=len(parts), alpha=alpha, tiled=tiled),
        grid=(M // tm, nk),
        in_specs=in_specs,
        out_specs=[pl.BlockSpec((tm, D), lambda i, k: (i, 0)), pl.BlockSpec((tm, D), lambda i, k: (i, 0))],
        out_shape=[jax.ShapeDtypeStruct((M, D), F32), jax.ShapeDtypeStruct((M, D), BF16)],
        scratch_shapes=[pltpu.VMEM((tm, D), F32)] if tiled else [],
        compiler_params=_cparams(("arbitrary", "arbitrary")),
        name="mm_post_norm",
    )(*parts, w3, resid, g3, b3)


def _mem_attn_kernel(x_ref, wq_ref, mk_ref, mv_ref, wo_ref, r_ref, g_ref, b_ref, of_ref, ob_ref, *, alpha):
    scale = HEAD_W ** -0.5
    q = (jnp.dot(x_ref[...], wq_ref[...], preferred_element_type=F32) * scale).astype(BF16)
    mk = mk_ref[...]
    mv = mv_ref[...]
    outs = []
    for h in range(N_HEADS_MEM):
        sl = slice(h * HEAD_W, (h + 1) * HEAD_W)
        s = lax.dot_general(q[:, sl], mk[:, sl], NT_DIMS, preferred_element_type=F32)
        s = s - jnp.max(s, axis=-1, keepdims=True)
        p = jnp.exp(s)
        o = jnp.dot(p.astype(BF16), mv[:, sl], preferred_element_type=F32)
        outs.append((o / jnp.sum(p, axis=-1, keepdims=True)).astype(BF16))
    o = jnp.concatenate(outs, axis=1)
    y = jnp.dot(o, wo_ref[...], preferred_element_type=F32)
    y = _post_norm(alpha * r_ref[...] + y, g_ref[...], b_ref[...])
    of_ref[...] = y
    ob_ref[...] = y.astype(BF16)


def _mem_attention(xb, w_mq, mkb, mvb, w_mo, l, resid, g3, b3, alpha, tm):
    M, D = resid.shape
    W = w_mq.shape[2]
    NM = mkb.shape[0]
    return pl.pallas_call(
        functools.partial(_mem_attn_kernel, alpha=alpha),
        grid=(M // tm,),
        in_specs=[pl.BlockSpec((tm, D), lambda i: (i, 0)),
                  pl.BlockSpec((None, D, W), lambda i: (l, 0, 0)),
                  pl.BlockSpec((NM, W), lambda i: (0, 0)),
                  pl.BlockSpec((NM, W), lambda i: (0, 0)),
                  pl.BlockSpec((None, W, D), lambda i: (l, 0, 0)),
                  pl.BlockSpec((tm, D), lambda i: (i, 0)),
                  pl.BlockSpec((None, None, 1, D), lambda i: (l, 1, 0, 0)),
                  pl.BlockSpec((None, None, 1, D), lambda i: (l, 1, 0, 0))],
        out_specs=[pl.BlockSpec((tm, D), lambda i: (i, 0)), pl.BlockSpec((tm, D), lambda i: (i, 0))],
        out_shape=[jax.ShapeDtypeStruct((M, D), F32), jax.ShapeDtypeStruct((M, D), BF16)],
        compiler_params=_cparams(("arbitrary",)),
        name="mem_attn",
    )(xb, w_mq, mkb, mvb, w_mo, resid, g3, b3)


def _swiglu_kernel(x_ref, wg_ref, wu_ref, o_ref):
    x = x_ref[...]
    g = jnp.dot(x, wg_ref[...], preferred_element_type=F32)
    u = jnp.dot(x, wu_ref[...], preferred_element_type=F32)
    o_ref[...] = (g * jax.nn.sigmoid(g) * u).astype(o_ref.dtype)


def _swiglu_up(xb, w_ff_in, l, tm, tn):
    M, D = xb.shape
    dff = w_ff_in.shape[2] // 2
    nj = dff // tn
    return pl.pallas_call(
        _swiglu_kernel,
        grid=(nj, M // tm),
        in_specs=[pl.BlockSpec((tm, D), lambda j, i: (i, 0)),
                  pl.BlockSpec((None, D, tn), lambda j, i: (l, 0, j)),
                  pl.BlockSpec((None, D, tn), lambda j, i: (l, 0, j + nj))],
        out_specs=pl.BlockSpec((tm, tn), lambda j, i: (i, j)),
        out_shape=jax.ShapeDtypeStruct((M, dff), BF16),
        compiler_params=_cparams(("arbitrary", "arbitrary")),
        name="swiglu_up",
    )(xb, w_ff_in, w_ff_in)


def _lane_suffix_tile(x):
    row = lax.broadcasted_iota(jnp.int32, x.shape, 0)
    lane = lax.broadcasted_iota(jnp.int32, x.shape, 1)
    y = x
    z = x
    for sh in (8, 16, 32, 64):
        shifted = pltpu.roll(y, LANES - sh, 1)
        y = y + jnp.where(lane + sh < LANES, shifted, 0.0)
        z = z + pltpu.roll(z, sh, 1)
    a = z
    for sh in (1, 2, 4):
        shifted = pltpu.roll(a, 8 - sh, 0)
        a = a + jnp.where(row + sh < 8, shifted, 0.0)
    return (y - x) + (a - z), a[0:1, :]


def _decode_kernel(pt_ref, qd_ref, qf_ref, kdn_ref, vdn_ref, kfn_ref, vfn_ref, lfn_ref,
                   blast_ref, bnew_ref, lam_ref, g_ref, *refs, n_pages, lam0):
    P = n_pages
    kd, vd, kf, vf, lfp = (refs[i * P:(i + 1) * P] for i in range(5))
    od_ref, of_ref = refs[5 * P:5 * P + 2]
    qd_sc, qf_sc, md, ld, accd, mf, lf_sum, accf, carry = refs[5 * P + 2:]
    c = pl.program_id(1)
    H = N_HEADS
    PS = kd[0].shape[0]
    R = PS * H

    @pl.when(c == 0)
    def _():
        q = qd_ref[...].astype(F32)
        lane = lax.broadcasted_iota(jnp.int32, q.shape, 1)
        qq = jnp.concatenate([jnp.where(lane < DH_DIFF, q, 0.0), jnp.where(lane >= DH_DIFF, q, 0.0)], axis=0)
        qd_sc[...] = qq.astype(BF16)
        qf_sc[...] = qf_ref[...]
        kn = kdn_ref[...].astype(BF16).astype(F32)
        m0 = jnp.sum(qq * jnp.concatenate([kn, kn], axis=0), axis=-1, keepdims=True) + bnew_ref[...][:, 0:1]
        md[...] = jnp.broadcast_to(m0, md.shape)
        ld[...] = jnp.full_like(ld, 1.0 / LANES)
        vn = vdn_ref[...].astype(BF16).astype(F32)
        accd[...] = jnp.concatenate([vn, vn], axis=0)
        m0 = jnp.sum(qf_ref[...].astype(F32) * kfn_ref[...].astype(BF16).astype(F32), axis=-1, keepdims=True)
        mf[...] = jnp.broadcast_to(m0, mf.shape)
        lf_sum[...] = jnp.full_like(lf_sum, 1.0 / LANES)
        accf[...] = vfn_ref[...].astype(BF16).astype(F32)
        lane1 = lax.broadcasted_iota(jnp.int32, (1, LANES), 1)
        v = jnp.where(lane1 < H, lfn_ref[...], 0.0)
        for sh in (8, 16, 32, 64):
            v = v + pltpu.roll(v, sh, 1)
        carry[...] = v

    def own_head(rows):
        r = lax.broadcasted_iota(jnp.int32, (rows, R), 0)
        col = lax.broadcasted_iota(jnp.int32, (rows, R), 1)
        return jnp.bitwise_and(col, H - 1) == jnp.bitwise_and(r, H - 1)

    own_d = own_head(2 * H)
    own_f = own_head(H)
    is_last = (c == 0).astype(F32)

    dec = [None] * P
    cv = carry[...]
    for r in range(P - 1, -1, -1):
        inner, tot = _lane_suffix_tile(lfp[r][...])
        tile = (inner + cv) * LOG2E
        dec[r] = jnp.concatenate([tile[i:i + 1, :] for i in range(8)], axis=1)
        cv = cv + tot
    carry[...] = cv

    sd, sf = [], []
    for r in range(P):
        s = lax.dot_general(qd_sc[...], kd[r][...].reshape(R, HEAD_W).astype(BF16), NT_DIMS,
                            preferred_element_type=F32)
        if r == P - 1:
            s = s + is_last * blast_ref[...]
        sd.append(jnp.where(own_d, s, NEG))
        s = lax.dot_general(qf_sc[...], kf[r][...].reshape(R, HEAD_W).astype(BF16), NT_DIMS,
                            preferred_element_type=F32) + dec[r]
        sf.append(jnp.where(own_f, s, NEG))

    def update(s_pages, v_refs, m_sc, l_sc, acc_sc):
        m_prev = m_sc[...]
        m_cur = s_pages[0]
        for s in s_pages[1:]:
            m_cur = jnp.maximum(m_cur, s)
        m_new = jnp.maximum(m_prev, jnp.max(m_cur, axis=-1, keepdims=True))
        alpha = jnp.exp2(m_prev - m_new)
        m_sc[...] = m_new
        m_wide = jnp.concatenate([m_new] * (R // LANES), axis=1)
        acc = alpha * acc_sc[...]
        lsum = alpha * l_sc[...]
        for s, v_ref in zip(s_pages, v_refs):
            p = jnp.exp2(s - m_wide)
            for i in range(R // LANES):
                lsum = lsum + p[:, i * LANES:(i + 1) * LANES]
            acc = acc + jnp.dot(p.astype(BF16), v_ref[...].reshape(R, HEAD_W).astype(BF16),
                                preferred_element_type=F32)
        l_sc[...] = lsum
        acc_sc[...] = acc

    update(sd, vd, md, ld, accd)
    update(sf, vf, mf, lf_sum, accf)

    @pl.when(c == pl.num_programs(1) - 1)
    def _():
        lam = _lam_value(lam_ref[...], lam0)
        o = accd[...] / jnp.sum(ld[...], axis=-1, keepdims=True)
        o = o[:H] - lam * o[H:]
        o = o * lax.rsqrt(jnp.mean(o * o, axis=-1, keepdims=True) + LN_EPS)
        od_ref[...] = (o * g_ref[...] * (1.0 - lam0)).astype(od_ref.dtype)
        of_ref[...] = (accf[...] / jnp.sum(lf_sum[...], axis=-1, keepdims=True)).astype(of_ref.dtype)


def _decode_attention(page_table, zsb, new_kv, lf_new, caches, bias_last, bias_new, lam_p, g, l, lam0, n_pages):
    B = zsb.shape[0]
    H = N_HEADS
    ckd, cvd, ckf, cvf, clf = caches
    PS = ckd.shape[2]
    per_seq = page_table.shape[1]
    nch = per_seq // n_pages

    def group_spec(g_idx):
        return pl.BlockSpec((None, H, HEAD_W), lambda b, c, pt: (b, g_idx, 0))

    new_spec = pl.BlockSpec((None, None, H, HEAD_W), lambda b, c, pt: (l, b, 0, 0))
    in_specs = [group_spec(0), group_spec(3), new_spec, new_spec, new_spec, new_spec,
                pl.BlockSpec((None, 1, LANES), lambda b, c, pt: (b, 0, 0)),
                pl.BlockSpec((1, PS * H), lambda b, c, pt: (0, 0)),
                pl.BlockSpec((2 * H, LANES), lambda b, c, pt: (0, 0)),
                pl.BlockSpec((None, 4, DH_DIFF), lambda b, c, pt: (l, 0, 0)),
                pl.BlockSpec((None, 1, HEAD_W), lambda b, c, pt: (l, 0, 0))]
    args = [zsb, zsb, *new_kv, lf_new.reshape(B, 1, LANES), bias_last, bias_new, lam_p, g]
    for cache in (ckd, cvd, ckf, cvf):
        for r in range(n_pages):
            in_specs.append(pl.BlockSpec((None, None, PS, H, HEAD_W),
                                         lambda b, c, pt, r=r: (l, pt[b, (nch - 1 - c) * n_pages + r], 0, 0, 0)))
            args.append(cache)
    for r in range(n_pages):
        in_specs.append(pl.BlockSpec((None, None, 8, LANES),
                                     lambda b, c, pt, r=r: (l, pt[b, (nch - 1 - c) * n_pages + r], 0, 0)))
        args.append(clf)
    out_spec = pl.BlockSpec((None, H, HEAD_W), lambda b, c, pt: (b, 0, 0))
    grid_spec = pltpu.PrefetchScalarGridSpec(
        num_scalar_prefetch=1, grid=(B, nch), in_specs=in_specs, out_specs=[out_spec, out_spec],
        scratch_shapes=[pltpu.VMEM((2 * H, HEAD_W), BF16), pltpu.VMEM((H, HEAD_W), BF16),
                        pltpu.VMEM((2 * H, LANES), F32), pltpu.VMEM((2 * H, LANES), F32),
                        pltpu.VMEM((2 * H, HEAD_W), F32),
                        pltpu.VMEM((H, LANES), F32), pltpu.VMEM((H, LANES), F32), pltpu.VMEM((H, HEAD_W), F32),
                        pltpu.VMEM((1, LANES), F32)])
    return pl.pallas_call(
        functools.partial(_decode_kernel, n_pages=n_pages, lam0=lam0),
        grid_spec=grid_spec,
        out_shape=[jax.ShapeDtypeStruct((B, H, HEAD_W), BF16), jax.ShapeDtypeStruct((B, H, HEAD_W), BF16)],
        compiler_params=_cparams(("arbitrary", "arbitrary")),
        name="decode_attn",
    )(page_table, *args)


def _mem_decode_kernel(q_ref, mk_ref, mv_ref, o_ref):
    q = q_ref[...].astype(F32)
    s = jnp.sum(mk_ref[...] * q[None], axis=-1, keepdims=True)
    p = jnp.exp(s - jnp.max(s, axis=0, keepdims=True))
    o = jnp.sum(p * mv_ref[...], axis=0) / jnp.sum(p, axis=0)
    o_ref[...] = o.astype(o_ref.dtype)


def _mem_decode_attention(qb, cmk, cmv, l):
    B, W = qb.shape
    NM, HM = cmk.shape[2:4]
    out = pl.pallas_call(
        _mem_decode_kernel,
        grid=(B,),
        in_specs=[pl.BlockSpec((None, HM, HEAD_W), lambda b: (b, 0, 0)),
                  pl.BlockSpec((None, None, NM, HM, HEAD_W), lambda b: (l, b, 0, 0, 0)),
                  pl.BlockSpec((None, None, NM, HM, HEAD_W), lambda b: (l, b, 0, 0, 0))],
        out_specs=pl.BlockSpec((None, HM, HEAD_W), lambda b: (b, 0, 0)),
        out_shape=jax.ShapeDtypeStruct((B, HM, HEAD_W), BF16),
        compiler_params=_cparams(("arbitrary",)),
        name="mem_decode_attn",
    )(qb.reshape(B, HM, HEAD_W), cmk, cmv)
    return out.reshape(B, W)


def _bias_by_distance(rel_bias, dist):
    bucket = _rel_bucket(dist)
    rel = (rel_bias - rel_bias[N_BUCKETS - 1]) * LOG2E
    out = jnp.zeros((rel_bias.shape[1],) + dist.shape, F32)
    for b in range(N_BUCKETS - 1):
        out = jnp.where((bucket == b)[None], rel[b].reshape((-1,) + (1,) * dist.ndim), out)
    return out


def _bias_tables(rel_bias, blk):
    assert blk >= MAX_DISTANCE
    r = jnp.arange(blk, dtype=jnp.int32)[:, None]
    c = jnp.arange(blk, dtype=jnp.int32)[None, :]
    diag = jnp.where((r >= c)[None], _bias_by_distance(rel_bias, jnp.maximum(r - c, 0)), NEG)
    sub = _bias_by_distance(rel_bias, blk + r - c)
    return diag, sub


def _decode_bias(rel_bias, page_size):
    assert page_size >= MAX_DISTANCE
    last = _bias_by_distance(rel_bias, page_size - jnp.arange(page_size, dtype=jnp.int32))
    new = jnp.broadcast_to(_bias_by_distance(rel_bias, jnp.zeros((1,), jnp.int32)), (N_HEADS, LANES))
    return last.T.reshape(1, page_size * N_HEADS), jnp.concatenate([new, new], axis=0)


def kernel(x_prompt, x_sample, cache_diff_k, cache_diff_v, cache_fox_k, cache_fox_v, cache_fox_logf,
           cache_mem_k, cache_mem_v, page_table, mem_prompt, w_in, b_forget, diff_lambda, diff_subln_g,
           rel_bias, w_o, w_mq, w_mkv, w_mo, w_ff_in, w_ff_out, ln_g, ln_b):
    depth = w_in.shape[0]
    alpha = (2 * depth) ** 0.25
    _, T, D = x_prompt.shape
    DB = x_sample.shape[0]
    W = N_HEADS * HEAD_W
    n_qkv = 6 * W
    pool, page_size = cache_diff_k.shape[1:3]
    n_mem = mem_prompt.shape[1]
    w_mem = w_mq.shape[2]
    blk = min(ATT_BLK, T)
    blk_fox = 2 * blk if T % (2 * blk) == 0 else blk
    tm = min(512, T)
    assert rel_bias.shape == (N_BUCKETS, N_HEADS) and w_in.shape[2] == n_qkv + N_HEADS

    w_in_b = w_in.astype(BF16)
    w_fl_b = jnp.pad(w_in[:, :, n_qkv:], ((0, 0), (0, 0), (0, LANES - N_HEADS))).astype(BF16)
    b_fl = jnp.pad(b_forget, ((0, 0), (0, LANES - N_HEADS))).reshape(depth, 1, LANES)
    w_o_b, w_mq_b, w_mkv_b, w_mo_b = (w.astype(BF16) for w in (w_o, w_mq, w_mkv, w_mo))
    w_ff_in_b, w_ff_out_b = w_ff_in.astype(BF16), w_ff_out.astype(BF16)
    ln_g4 = ln_g.reshape(depth, 3, 1, D)
    ln_b4 = ln_b.reshape(depth, 3, 1, D)
    g_sub = diff_subln_g.reshape(depth, 1, HEAD_W)
    qk_scales = (DH_DIFF ** -0.5 * LOG2E, 1.0, 1.0, HEAD_W ** -0.5 * LOG2E, 1.0, 1.0)
    bias_diag, bias_sub = _bias_tables(rel_bias, blk)
    bias_last, bias_new = _decode_bias(rel_bias, page_size)
    d_ff = w_ff_out.shape[1]
    tn_ff = 512 if d_ff % 512 == 0 else d_ff
    tk_ff = d_ff // 4 if d_ff % (4 * LANES) == 0 else d_ff

    def ffn(xf, xb, l, tmm, tm_up, tk):
        hff = _swiglu_up(xb, w_ff_in_b, l, tm_up, tn_ff)
        return _mm_post_norm([hff], w_ff_out_b, l, xf, ln_g4, ln_b4, 2, alpha, tmm, tk)

    xf = x_prompt.reshape(T, D)
    xb = xf.astype(BF16)
    memb = mem_prompt.reshape(n_mem, D).astype(BF16)
    p_kv, p_lf, p_mk, p_mv = None, [], [], []
    for l in range(depth):
        lam0 = _lambda_init(l)
        alt = l == 1
        *p_kv, zb = _in_proj(xb, w_in_b, l, min(T, 2 * tm) if alt else tm, qk_scales, p_kv)
        lf, ct = _logits(xb, w_fl_b, b_fl, l, tm, True)
        od = _diff_attention(zb, bias_diag, bias_sub, diff_lambda, g_sub, l, lam0, blk, bf16_exp=True)
        of = _fox_attention(zb, ct, blk_fox, bf16_exp=alt)
        xf, xb = _mm_post_norm([od, of], w_o_b, l, xf, ln_g4, ln_b4, 0, alpha, tm)
        mkv_f, mkv_b = _proj(memb, w_mkv_b, l, w_mem, n_mem, want_f32=True)
        xf, xb = _mem_attention(xb, w_mq_b, mkv_b[:, :w_mem], mkv_b[:, w_mem:], w_mo_b, l, xf, ln_g4, ln_b4, alpha, tm)
        xf, xb = ffn(xf, xb, l, tm, min(T, 2 * tm), 2 * tk_ff if (alt and d_ff % (2 * tk_ff) == 0) else tk_ff)
        p_lf.append(lf[:, :N_HEADS])
        p_mk.append(mkv_f[:, :w_mem])
        p_mv.append(mkv_f[:, w_mem:])
    y_prompt = xf.reshape(1, T, D)

    assert page_size * N_HEADS == 8 * LANES
    caches = (cache_diff_k, cache_diff_v, cache_fox_k, cache_fox_v, cache_fox_logf.reshape(depth, pool, 8, LANES))
    per_seq = page_table.shape[1]
    n_pages = 8 if per_seq % 8 == 0 else per_seq
    xf = x_sample.reshape(DB, D)
    xb = xf.astype(BF16)
    s_kv, s_lf = None, []
    for l in range(depth):
        lam0 = _lambda_init(l)
        *s_kv, zb = _in_proj(xb, w_in_b, l, DB, qk_scales, s_kv)
        (lf,) = _logits(xb, w_fl_b, b_fl, l, DB, False)
        od, of = _decode_attention(page_table, jnp.swapaxes(zb, 0, 1), s_kv, lf, caches,
                                   bias_last, bias_new, diff_lambda, g_sub, l, lam0, n_pages)
        xf, xb = _mm_post_norm([od.reshape(DB, W), of.reshape(DB, W)], w_o_b, l, xf, ln_g4, ln_b4, 0, alpha, DB)
        (qm,) = _proj(xb, w_mq_b, l, w_mem, DB, scale=HEAD_W ** -0.5)
        om = _mem_decode_attention(qm, cache_mem_k, cache_mem_v, l)
        xf, xb = _mm_post_norm([om], w_mo_b, l, xf, ln_g4, ln_b4, 1, alpha, DB)
        xf, xb = ffn(xf, xb, l, DB, DB, tk_ff)
        s_lf.append(lf[:, :N_HEADS])
    y_sample = xf.reshape(DB, 1, D)

    def st(parts, *shape):
        return jnp.stack(parts).reshape(depth, *shape)

    return (y_prompt, y_sample,
            *(a.reshape(depth, 1, T, N_HEADS, HEAD_W) for a in p_kv), st(p_lf, 1, T, N_HEADS),
            st(p_mk, 1, n_mem, N_HEADS_MEM, HEAD_W), st(p_mv, 1, n_mem, N_HEADS_MEM, HEAD_W),
            *(a.reshape(depth, DB, 1, N_HEADS, HEAD_W) for a in s_kv), st(s_lf, DB, 1, N_HEADS))
```

```python
import functools
import math

import jax
import jax.numpy as jnp
from jax import lax
from jax.experimental import pallas as pl
from jax.experimental.pallas import tpu as pltpu

F32 = jnp.float32
BF16 = jnp.bfloat16

LANES = 128
LOG2E = 1.4426950408889634
NEG = -1e30
LN_EPS = 1e-5
N_BUCKETS = 32
MAX_DISTANCE = 128
DH_DIFF = 64
HEAD_W = 128
N_HEADS = 8
N_HEADS_MEM = 4
ATT_BLK = 512
VMEM_LIMIT = 56 * 1024 * 1024

NT_DIMS = (((1,), (1,)), ((), ()))


def _cparams(sem):
    return pltpu.CompilerParams(dimension_semantics=sem, vmem_limit_bytes=VMEM_LIMIT)


def _lambda_init(layer):
    return 0.8 - 0.6 * math.exp(-0.3 * layer)


def _rel_bucket(dist):
    n = jnp.maximum(dist, 0)
    max_exact = N_BUCKETS // 2
    nf = jnp.maximum(n, 1).astype(F32)
    large = max_exact + (jnp.log(nf / max_exact) / math.log(MAX_DISTANCE / max_exact)
                         * (N_BUCKETS - max_exact)).astype(jnp.int32)
    large = jnp.minimum(large, N_BUCKETS - 1)
    return jnp.where(n < max_exact, n, large)


def _split3(x):
    hi = x.astype(BF16)
    r1 = x - hi.astype(F32)
    mid = r1.astype(BF16)
    lo = (r1 - mid.astype(F32)).astype(BF16)
    return hi, mid, lo


def _lane_cat(a):
    return jnp.concatenate([a[h] for h in range(a.shape[0])], axis=1)


def _post_norm(h, g, b):
    mu = jnp.mean(h, axis=-1, keepdims=True)
    d = h - mu
    var = jnp.mean(d * d, axis=-1, keepdims=True)
    return d * lax.rsqrt(var + LN_EPS) * g + b


def _lam_value(lp, lam0):
    a = jnp.sum(lp[0:1] * lp[1:2], axis=-1, keepdims=True)
    b = jnp.sum(lp[2:3] * lp[3:4], axis=-1, keepdims=True)
    return jnp.exp(a) - jnp.exp(b) + lam0


def _proj_kernel(x_ref, w_ref, *out_refs, scale, want_f32):
    acc = jnp.dot(x_ref[...], w_ref[...], preferred_element_type=F32)
    if want_f32:
        out_refs[0][...] = acc
    out_refs[-1][...] = (acc * scale).astype(BF16)


def _proj(x, w3, l, tn, tm, scale=1.0, want_f32=False):
    M, K = x.shape
    N = w3.shape[2]
    spec = pl.BlockSpec((tm, tn), lambda j, i: (i, j))
    dtypes = ([F32] if want_f32 else []) + [BF16]
    return pl.pallas_call(
        functools.partial(_proj_kernel, scale=scale, want_f32=want_f32),
        grid=(N // tn, M // tm),
        in_specs=[pl.BlockSpec((tm, K), lambda j, i: (i, 0)),
                  pl.BlockSpec((None, K, tn), lambda j, i: (l, 0, j))],
        out_specs=[spec] * len(dtypes), out_shape=[jax.ShapeDtypeStruct((M, N), d) for d in dtypes],
        compiler_params=_cparams(("arbitrary", "arbitrary")),
        name="proj",
    )(x, w3)


F32_GROUPS = (1, 2, 4, 5)


def _in_proj_kernel(x_ref, w_ref, *refs, scales):
    dk_ref, dv_ref, fk_ref, fv_ref, zb_ref = refs[-5:]
    f32_out = dict(zip(F32_GROUPS, (dk_ref, dv_ref, fk_ref, fv_ref)))
    j = pl.program_id(1)

    def group(g):
        acc = jnp.dot(x_ref[...], w_ref[...], preferred_element_type=F32)
        tm = acc.shape[0]
        if g in f32_out:
            f32_out[g][...] = acc.reshape(tm, N_HEADS, HEAD_W)
        y = (acc * scales[g]).astype(BF16) if scales[g] != 1.0 else acc.astype(BF16)
        for hh in range(N_HEADS):
            zb_ref[hh] = y[:, hh * HEAD_W:(hh + 1) * HEAD_W]

    for g in range(len(scales)):
        pl.when(j == g)(functools.partial(group, g))


def _in_proj(x, w3, l, tm, scales, prev=None):
    M, K = x.shape
    depth = w3.shape[0]
    W = N_HEADS * HEAD_W
    f32_spec = pl.BlockSpec((None, tm, N_HEADS, HEAD_W), lambda i, j: (l, i, 0, 0))
    f32_shape = jax.ShapeDtypeStruct((depth, M, N_HEADS, HEAD_W), F32)
    prev = list(prev) if prev is not None else []
    return pl.pallas_call(
        functools.partial(_in_proj_kernel, scales=tuple(scales)),
        grid=(M // tm, len(scales)),
        in_specs=[pl.BlockSpec((tm, K), lambda i, j: (i, 0)),
                  pl.BlockSpec((None, K, W), lambda i, j: (l, 0, j))]
                 + [pl.BlockSpec(memory_space=pl.ANY)] * len(prev),
        out_specs=[f32_spec] * 4 + [pl.BlockSpec((N_HEADS, tm, HEAD_W), lambda i, j: (j, i, 0))],
        out_shape=[f32_shape] * 4 + [jax.ShapeDtypeStruct((len(scales) * N_HEADS, M, HEAD_W), BF16)],
        input_output_aliases={2 + k: k for k in range(len(prev))},
        compiler_params=_cparams(("arbitrary", "arbitrary")),
        name="in_proj",
    )(x, w3, *prev)


def _logit_kernel(x_ref, w_ref, b_ref, lf_ref, *rest, n_valid, with_cumsum):
    fl = jnp.dot(x_ref[...], w_ref[...], preferred_element_type=F32) + b_ref[...]
    lf = jnp.minimum(fl, 0.0) - jnp.log1p(jnp.exp(-jnp.abs(fl)))
    tm = lf.shape[0]
    lane = lax.broadcasted_iota(jnp.int32, lf.shape, 1)
    lf = jnp.where(lane < n_valid, lf, 0.0)
    lf_ref[...] = lf
    if with_cumsum:
        ct_ref, carry = rest

        @pl.when(pl.program_id(0) == 0)
        def _():
            carry[...] = jnp.zeros_like(carry)

        row = lax.broadcasted_iota(jnp.int32, (tm, tm), 0)
        col = lax.broadcasted_iota(jnp.int32, (tm, tm), 1)
        tri = (row >= col).astype(BF16)
        hi, mid, lo = _split3(lf)
        c = (jnp.dot(tri, hi, preferred_element_type=F32)
             + jnp.dot(tri, mid, preferred_element_type=F32)
             + jnp.dot(tri, lo, preferred_element_type=F32)) + carry[...]
        ct_ref[...] = c.T[:ct_ref.shape[0], :]
        carry[...] = c[tm - 1:tm, :]


def _logits(x, wl3, bl3, l, tm, with_cumsum):
    M, K = x.shape
    out_shape = [jax.ShapeDtypeStruct((M, LANES), F32)]
    out_specs = [pl.BlockSpec((tm, LANES), lambda i: (i, 0))]
    scratch = []
    if with_cumsum:
        out_shape += [jax.ShapeDtypeStruct((N_HEADS, M), F32)]
        out_specs += [pl.BlockSpec((N_HEADS, tm), lambda i: (0, i))]
        scratch = [pltpu.VMEM((1, LANES), F32)]
    return pl.pallas_call(
        functools.partial(_logit_kernel, n_valid=N_HEADS, with_cumsum=with_cumsum),
        grid=(M // tm,),
        in_specs=[pl.BlockSpec((tm, K), lambda i: (i, 0)),
                  pl.BlockSpec((None, K, LANES), lambda i: (l, 0, 0)),
                  pl.BlockSpec((None, 1, LANES), lambda i: (l, 0, 0))],
        out_specs=out_specs, out_shape=out_shape, scratch_shapes=scratch,
        compiler_params=_cparams(("arbitrary",)),
        name="logits",
    )(x, wl3, bl3)


def _flash_init(m_sc, l_sc, acc_sc):
    m_sc[...] = jnp.full_like(m_sc, NEG)
    l_sc[...] = jnp.zeros_like(l_sc)
    acc_sc[...] = jnp.zeros_like(acc_sc)


def _flash_update(s, v, m_sc, l_sc, acc_sc, bf16_exp=False):
    n = s.shape[1] // LANES
    m_prev = m_sc[...]
    m_new = jnp.maximum(m_prev, jnp.max(s, axis=-1, keepdims=True))
    alpha = jnp.exp2(m_prev - m_new)
    z = s - jnp.concatenate([m_new] * n, axis=1)
    p = jnp.exp2(z.astype(BF16)) if bf16_exp else jnp.exp2(z)
    psum = p[:, :LANES]
    for c in range(1, n):
        psum = psum + p[:, c * LANES:(c + 1) * LANES]
    l_sc[...] = alpha * l_sc[...] + psum.astype(F32)
    acc_sc[...] = alpha * acc_sc[...] + jnp.dot(p.astype(BF16), v, preferred_element_type=F32)
    m_sc[...] = m_new


def _flash_rowsum(l_sc):
    return jnp.sum(l_sc[...], axis=-1, keepdims=True)


def _diff_attn_kernel(q_ref, k_ref, v_ref, bd_ref, bs_ref, lam_ref, g_ref, o_ref,
                      m_sc, l_sc, acc_sc, *, blk, lam0, bf16_exp):
    qi = pl.program_id(1)
    q = q_ref[...].astype(F32)
    lane = lax.broadcasted_iota(jnp.int32, q.shape, 1)
    qq = jnp.concatenate([jnp.where(lane < DH_DIFF, q, 0.0), jnp.where(lane >= DH_DIFF, q, 0.0)], axis=0).astype(BF16)
    _flash_init(m_sc, l_sc, acc_sc)

    def step(kj, bias_ref):
        ks = pl.multiple_of(kj * blk, blk)
        k = k_ref[pl.ds(ks, blk), :]
        v = v_ref[pl.ds(ks, blk), :]
        s = lax.dot_general(qq, k, NT_DIMS, preferred_element_type=F32)
        if bias_ref is not None:
            b = bias_ref[...]
            s = s + jnp.concatenate([b, b], axis=0)
        _flash_update(s, v, m_sc, l_sc, acc_sc, bf16_exp)

    def far_body(kj, carry):
        step(kj, None)
        return carry

    lax.fori_loop(0, jnp.maximum(qi - 1, 0), far_body, 0)

    @pl.when(qi >= 1)
    def _():
        step(qi - 1, bs_ref)

    step(qi, bd_ref)

    lam = _lam_value(lam_ref[...], lam0)
    o = acc_sc[...] / _flash_rowsum(l_sc)
    o = o[:blk] - lam * o[blk:]
    o = o * lax.rsqrt(jnp.mean(o * o, axis=-1, keepdims=True) + LN_EPS)
    o_ref[...] = (o * g_ref[...] * (1.0 - lam0)).astype(o_ref.dtype)


def _diff_attention(zb, bias_diag, bias_sub, lam_p, g, l, lam0, blk, bf16_exp=False):
    T = zb.shape[1]
    H = N_HEADS
    return pl.pallas_call(
        functools.partial(_diff_attn_kernel, blk=blk, lam0=lam0, bf16_exp=bf16_exp),
        grid=(H, T // blk),
        in_specs=[pl.BlockSpec((None, blk, LANES), lambda h, i: (h, i, 0)),
                  pl.BlockSpec((None, T, LANES), lambda h, i: (H + h, 0, 0)),
                  pl.BlockSpec((None, T, LANES), lambda h, i: (2 * H + h, 0, 0)),
                  pl.BlockSpec((None, blk, blk), lambda h, i: (h, 0, 0)),
                  pl.BlockSpec((None, blk, blk), lambda h, i: (h, 0, 0)),
                  pl.BlockSpec((None, 4, DH_DIFF), lambda h, i: (l, 0, 0)),
                  pl.BlockSpec((None, 1, LANES), lambda h, i: (l, 0, 0))],
        out_specs=pl.BlockSpec((None, blk, LANES), lambda h, i: (h, i, 0)),
        out_shape=jax.ShapeDtypeStruct((H, T, LANES), BF16),
        scratch_shapes=[pltpu.VMEM((2 * blk, LANES), F32), pltpu.VMEM((2 * blk, LANES), F32),
                        pltpu.VMEM((2 * blk, LANES), F32)],
        compiler_params=_cparams(("arbitrary", "arbitrary")),
        name="diff_attn",
    )(zb, zb, zb, bias_diag, bias_sub, lam_p, g)


def _fox_attn_kernel(q_ref, k_ref, v_ref, ct_ref, o_ref, m_sc, l_sc, acc_sc, *, blk, bf16_exp):
    h = pl.program_id(0)
    qi = pl.program_id(1)
    q = q_ref[...]
    qs = pl.multiple_of(qi * blk, blk)
    c_ref0 = ct_ref[pl.ds(h, 1), pl.ds(qs, blk)][:, 0:1]
    _flash_init(m_sc, l_sc, acc_sc)

    def step(kj, masked):
        ks = pl.multiple_of(kj * blk, blk)
        k = k_ref[pl.ds(ks, blk), :]
        v = v_ref[pl.ds(ks, blk), :]
        ck = (ct_ref[pl.ds(h, 1), pl.ds(ks, blk)] - c_ref0) * LOG2E
        s = lax.dot_general(q, k, NT_DIMS, preferred_element_type=F32) - ck
        if masked:
            row = lax.broadcasted_iota(jnp.int32, s.shape, 0)
            col = lax.broadcasted_iota(jnp.int32, s.shape, 1)
            s = jnp.where(row >= col, s, NEG)
        _flash_update(s, v, m_sc, l_sc, acc_sc, bf16_exp)

    def far_body(kj, carry):
        step(kj, False)
        return carry

    lax.fori_loop(0, qi, far_body, 0)
    step(qi, True)
    o_ref[...] = (acc_sc[...] / _flash_rowsum(l_sc)).astype(o_ref.dtype)


def _fox_attention(zb, ct, blk, bf16_exp=False):
    T = zb.shape[1]
    H = N_HEADS
    return pl.pallas_call(
        functools.partial(_fox_attn_kernel, blk=blk, bf16_exp=bf16_exp),
        grid=(H, T // blk),
        in_specs=[pl.BlockSpec((None, blk, LANES), lambda h, i: (3 * H + h, i, 0)),
                  pl.BlockSpec((None, T, LANES), lambda h, i: (4 * H + h, 0, 0)),
                  pl.BlockSpec((None, T, LANES), lambda h, i: (5 * H + h, 0, 0)),
                  pl.BlockSpec((N_HEADS, T), lambda h, i: (0, 0))],
        out_specs=pl.BlockSpec((None, blk, LANES), lambda h, i: (h, i, 0)),
        out_shape=jax.ShapeDtypeStruct((H, T, LANES), BF16),
        scratch_shapes=[pltpu.VMEM((blk, LANES), F32), pltpu.VMEM((blk, LANES), F32),
                        pltpu.VMEM((blk, LANES), F32)],
        compiler_params=_cparams(("arbitrary", "arbitrary")),
        name="fox_attn",
    )(zb, zb, zb, ct)


def _mm_pn_kernel(*refs, n_parts, alpha, tiled):
    a_refs = refs[:n_parts]
    w_ref, r_ref, g_ref, b_ref, of_ref, ob_ref = refs[n_parts:n_parts + 6]

    def finish(y):
        y = _post_norm(alpha * r_ref[...] + y, g_ref[...], b_ref[...])
        of_ref[...] = y
        ob_ref[...] = y.astype(BF16)

    if not tiled:
        y = None
        row = 0
        for a_ref in a_refs:
            a = a_ref[...]
            if a.ndim == 3:
                a = _lane_cat(a)
            d = jnp.dot(a, w_ref[row:row + a.shape[1], :], preferred_element_type=F32)
            y = d if y is None else y + d
            row += a.shape[1]
        finish(y)
        return

    acc = refs[n_parts + 6]
    k = pl.program_id(1)

    @pl.when(k == 0)
    def _():
        acc[...] = jnp.zeros_like(acc)

    acc[...] += jnp.dot(a_refs[0][...], w_ref[...], preferred_element_type=F32)

    @pl.when(k == pl.num_programs(1) - 1)
    def _():
        finish(acc[...])


def _mm_post_norm(parts, w3, l, resid, g3, b3, ln_idx, alpha, tm, tk=None):
    M, D = resid.shape
    tiled = tk is not None
    in_specs = []
    if tiled:
        assert len(parts) == 1 and parts[0].ndim == 2 and parts[0].shape[1] % tk == 0
        nk = parts[0].shape[1] // tk
        in_specs.append(pl.BlockSpec((tm, tk), lambda i, k: (i, k)))
        in_specs.append(pl.BlockSpec((None, tk, D), lambda i, k: (l, k, 0)))
    else:
        nk = 1
        for a in parts:
            if a.ndim == 3:
                in_specs.append(pl.BlockSpec((a.shape[0], tm, LANES), lambda i, k: (0, i, 0)))
            else:
                in_specs.append(pl.BlockSpec((tm, a.shape[1]), lambda i, k: (i, 0)))
        in_specs.append(pl.BlockSpec((None, w3.shape[1], D), lambda i, k: (l, 0, 0)))
    in_specs += [pl.BlockSpec((tm, D), lambda i, k: (i, 0)),
                 pl.BlockSpec((None, None, 1, D), lambda i, k: (l, ln_idx, 0, 0)),
                 pl.BlockSpec((None, None, 1, D), lambda i, k: (l, ln_idx, 0, 0))]
    return pl.pallas_call(
        functools.partial(_mm_pn_kernel, n_parts=len(parts), alpha=alpha, tiled=tiled),
        grid=(M // tm, nk),
        in_specs=in_specs,
        out_specs=[pl.BlockSpec((tm, D), lambda i, k: (i, 0)), pl.BlockSpec((tm, D), lambda i, k: (i, 0))],
        out_shape=[jax.ShapeDtypeStruct((M, D), F32), jax.ShapeDtypeStruct((M, D), BF16)],
        scratch_shapes=[pltpu.VMEM((tm, D), F32)] if tiled else [],
        compiler_params=_cparams(("arbitrary", "arbitrary")),
        name="mm_post_norm",
    )(*parts, w3, resid, g3, b3)


def _mem_attn_kernel(x_ref, wq_ref, mk_ref, mv_ref, wo_ref, r_ref, g_ref, b_ref, of_ref, ob_ref, *, alpha):
    scale = HEAD_W ** -0.5
    q = (jnp.dot(x_ref[...], wq_ref[...], preferred_element_type=F32) * scale).astype(BF16)
    mk = mk_ref[...]
    mv = mv_ref[...]
    outs = []
    for h in range(N_HEADS_MEM):
        sl = slice(h * HEAD_W, (h + 1) * HEAD_W)
        s = lax.dot_general(q[:, sl], mk[:, sl], NT_DIMS, preferred_element_type=F32)
        s = s - jnp.max(s, axis=-1, keepdims=True)
        p = jnp.exp(s)
        o = jnp.dot(p.astype(BF16), mv[:, sl], preferred_element_type=F32)
        outs.append((o / jnp.sum(p, axis=-1, keepdims=True)).astype(BF16))
    o = jnp.concatenate(outs, axis=1)
    y = jnp.dot(o, wo_ref[...], preferred_element_type=F32)
    y = _post_norm(alpha * r_ref[...] + y, g_ref[...], b_ref[...])
    of_ref[...] = y
    ob_ref[...] = y.astype(BF16)


def _mem_attention(xb, w_mq, mkb, mvb, w_mo, l, resid, g3, b3, alpha, tm):
    M, D = resid.shape
    W = w_mq.shape[2]
    NM = mkb.shape[0]
    return pl.pallas_call(
        functools.partial(_mem_attn_kernel, alpha=alpha),
        grid=(M // tm,),
        in_specs=[pl.BlockSpec((tm, D), lambda i: (i, 0)),
                  pl.BlockSpec((None, D, W), lambda i: (l, 0, 0)),
                  pl.BlockSpec((NM, W), lambda i: (0, 0)),
                  pl.BlockSpec((NM, W), lambda i: (0, 0)),
                  pl.BlockSpec((None, W, D), lambda i: (l, 0, 0)),
                  pl.BlockSpec((tm, D), lambda i: (i, 0)),
                  pl.BlockSpec((None, None, 1, D), lambda i: (l, 1, 0, 0)),
                  pl.BlockSpec((None, None, 1, D), lambda i: (l, 1, 0, 0))],
        out_specs=[pl.BlockSpec((tm, D), lambda i: (i, 0)), pl.BlockSpec((tm, D), lambda i: (i, 0))],
        out_shape=[jax.ShapeDtypeStruct((M, D), F32), jax.ShapeDtypeStruct((M, D), BF16)],
        compiler_params=_cparams(("arbitrary",)),
        name="mem_attn",
    )(xb, w_mq, mkb, mvb, w_mo, resid, g3, b3)


def _swiglu_kernel(x_ref, wg_ref, wu_ref, o_ref):
    x = x_ref[...]
    g = jnp.dot(x, wg_ref[...], preferred_element_type=F32)
    u = jnp.dot(x, wu_ref[...], preferred_element_type=F32)
    o_ref[...] = (g * jax.nn.sigmoid(g) * u).astype(o_ref.dtype)


def _swiglu_up(xb, w_ff_in, l, tm, tn):
    M, D = xb.shape
    dff = w_ff_in.shape[2] // 2
    nj = dff // tn
    return pl.pallas_call(
        _swiglu_kernel,
        grid=(nj, M // tm),
        in_specs=[pl.BlockSpec((tm, D), lambda j, i: (i, 0)),
                  pl.BlockSpec((None, D, tn), lambda j, i: (l, 0, j)),
                  pl.BlockSpec((None, D, tn), lambda j, i: (l, 0, j + nj))],
        out_specs=pl.BlockSpec((tm, tn), lambda j, i: (i, j)),
        out_shape=jax.ShapeDtypeStruct((M, dff), BF16),
        compiler_params=_cparams(("arbitrary", "arbitrary")),
        name="swiglu_up",
    )(xb, w_ff_in, w_ff_in)


def _lane_suffix_tile(x):
    row = lax.broadcasted_iota(jnp.int32, x.shape, 0)
    lane = lax.broadcasted_iota(jnp.int32, x.shape, 1)
    y = x
    z = x
    for sh in (8, 16, 32, 64):
        shifted = pltpu.roll(y, LANES - sh, 1)
        y = y + jnp.where(lane + sh < LANES, shifted, 0.0)
        z = z + pltpu.roll(z, sh, 1)
    a = z
    for sh in (1, 2, 4):
        shifted = pltpu.roll(a, 8 - sh, 0)
        a = a + jnp.where(row + sh < 8, shifted, 0.0)
    return (y - x) + (a - z), a[0:1, :]


def _decode_kernel(pt_ref, qd_ref, qf_ref, kdn_ref, vdn_ref, kfn_ref, vfn_ref, lfn_ref,
                   blast_ref, bnew_ref, lam_ref, g_ref, *refs, n_pages, lam0):
    P = n_pages
    kd, vd, kf, vf, lfp = (refs[i * P:(i + 1) * P] for i in range(5))
    od_ref, of_ref = refs[5 * P:5 * P + 2]
    qd_sc, qf_sc, md, ld, accd, mf, lf_sum, accf, carry = refs[5 * P + 2:]
    c = pl.program_id(1)
    H = N_HEADS
    PS = kd[0].shape[0]
    R = PS * H

    @pl.when(c == 0)
    def _():
        q = qd_ref[...].astype(F32)
        lane = lax.broadcasted_iota(jnp.int32, q.shape, 1)
        qq = jnp.concatenate([jnp.where(lane < DH_DIFF, q, 0.0), jnp.where(lane >= DH_DIFF, q, 0.0)], axis=0)
        qd_sc[...] = qq.astype(BF16)
        qf_sc[...] = qf_ref[...]
        kn = kdn_ref[...].astype(BF16).astype(F32)
        m0 = jnp.sum(qq * jnp.concatenate([kn, kn], axis=0), axis=-1, keepdims=True) + bnew_ref[...][:, 0:1]
        md[...] = jnp.broadcast_to(m0, md.shape)
        ld[...] = jnp.full_like(ld, 1.0 / LANES)
        vn = vdn_ref[...].astype(BF16).astype(F32)
        accd[...] = jnp.concatenate([vn, vn], axis=0)
        m0 = jnp.sum(qf_ref[...].astype(F32) * kfn_ref[...].astype(BF16).astype(F32), axis=-1, keepdims=True)
        mf[...] = jnp.broadcast_to(m0, mf.shape)
        lf_sum[...] = jnp.full_like(lf_sum, 1.0 / LANES)
        accf[...] = vfn_ref[...].astype(BF16).astype(F32)
        lane1 = lax.broadcasted_iota(jnp.int32, (1, LANES), 1)
        v = jnp.where(lane1 < H, lfn_ref[...], 0.0)
        for sh in (8, 16, 32, 64):
            v = v + pltpu.roll(v, sh, 1)
        carry[...] = v

    def own_head(rows):
        r = lax.broadcasted_iota(jnp.int32, (rows, R), 0)
        col = lax.broadcasted_iota(jnp.int32, (rows, R), 1)
        return jnp.bitwise_and(col, H - 1) == jnp.bitwise_and(r, H - 1)

    own_d = own_head(2 * H)
    own_f = own_head(H)
    is_last = (c == 0).astype(F32)

    dec = [None] * P
    cv = carry[...]
    for r in range(P - 1, -1, -1):
        inner, tot = _lane_suffix_tile(lfp[r][...])
        tile = (inner + cv) * LOG2E
        dec[r] = jnp.concatenate([tile[i:i + 1, :] for i in range(8)], axis=1)
        cv = cv + tot
    carry[...] = cv

    sd, sf = [], []
    for r in range(P):
        s = lax.dot_general(qd_sc[...], kd[r][...].reshape(R, HEAD_W).astype(BF16), NT_DIMS,
                            preferred_element_type=F32)
        if r == P - 1:
            s = s + is_last * blast_ref[...]
        sd.append(jnp.where(own_d, s, NEG))
        s = lax.dot_general(qf_sc[...], kf[r][...].reshape(R, HEAD_W).astype(BF16), NT_DIMS,
                            preferred_element_type=F32) + dec[r]
        sf.append(jnp.where(own_f, s, NEG))

    def update(s_pages, v_refs, m_sc, l_sc, acc_sc):
        m_prev = m_sc[...]
        m_cur = s_pages[0]
        for s in s_pages[1:]:
            m_cur = jnp.maximum(m_cur, s)
        m_new = jnp.maximum(m_prev, jnp.max(m_cur, axis=-1, keepdims=True))
        alpha = jnp.exp2(m_prev - m_new)
        m_sc[...] = m_new
        m_wide = jnp.concatenate([m_new] * (R // LANES), axis=1)
        acc = alpha * acc_sc[...]
        lsum = alpha * l_sc[...]
        for s, v_ref in zip(s_pages, v_refs):
            p = jnp.exp2(s - m_wide)
            for i in range(R // LANES):
                lsum = lsum + p[:, i * LANES:(i + 1) * LANES]
            acc = acc + jnp.dot(p.astype(BF16), v_ref[...].reshape(R, HEAD_W).astype(BF16),
                                preferred_element_type=F32)
        l_sc[...] = lsum
        acc_sc[...] = acc

    update(sd, vd, md, ld, accd)
    update(sf, vf, mf, lf_sum, accf)

    @pl.when(c == pl.num_programs(1) - 1)
    def _():
        lam = _lam_value(lam_ref[...], lam0)
        o = accd[...] / jnp.sum(ld[...], axis=-1, keepdims=True)
        o = o[:H] - lam * o[H:]
        o = o * lax.rsqrt(jnp.mean(o * o, axis=-1, keepdims=True) + LN_EPS)
        od_ref[...] = (o * g_ref[...] * (1.0 - lam0)).astype(od_ref.dtype)
        of_ref[...] = (accf[...] / jnp.sum(lf_sum[...], axis=-1, keepdims=True)).astype(of_ref.dtype)


def _decode_attention(page_table, zsb, new_kv, lf_new, caches, bias_last, bias_new, lam_p, g, l, lam0, n_pages):
    B = zsb.shape[0]
    H = N_HEADS
    ckd, cvd, ckf, cvf, clf = caches
    PS = ckd.shape[2]
    per_seq = page_table.shape[1]
    nch = per_seq // n_pages

    def group_spec(g_idx):
        return pl.BlockSpec((None, H, HEAD_W), lambda b, c, pt: (b, g_idx, 0))

    new_spec = pl.BlockSpec((None, None, H, HEAD_W), lambda b, c, pt: (l, b, 0, 0))
    in_specs = [group_spec(0), group_spec(3), new_spec, new_spec, new_spec, new_spec,
                pl.BlockSpec((None, 1, LANES), lambda b, c, pt: (b, 0, 0)),
                pl.BlockSpec((1, PS * H), lambda b, c, pt: (0, 0)),
                pl.BlockSpec((2 * H, LANES), lambda b, c, pt: (0, 0)),
                pl.BlockSpec((None, 4, DH_DIFF), lambda b, c, pt: (l, 0, 0)),
                pl.BlockSpec((None, 1, HEAD_W), lambda b, c, pt: (l, 0, 0))]
    args = [zsb, zsb, *new_kv, lf_new.reshape(B, 1, LANES), bias_last, bias_new, lam_p, g]
    for cache in (ckd, cvd, ckf, cvf):
        for r in range(n_pages):
            in_specs.append(pl.BlockSpec((None, None, PS, H, HEAD_W),
                                         lambda b, c, pt, r=r: (l, pt[b, (nch - 1 - c) * n_pages + r], 0, 0, 0)))
            args.append(cache)
    for r in range(n_pages):
        in_specs.append(pl.BlockSpec((None, None, 8, LANES),
                                     lambda b, c, pt, r=r: (l, pt[b, (nch - 1 - c) * n_pages + r], 0, 0)))
        args.append(clf)
    out_spec = pl.BlockSpec((None, H, HEAD_W), lambda b, c, pt: (b, 0, 0))
    grid_spec = pltpu.PrefetchScalarGridSpec(
        num_scalar_prefetch=1, grid=(B, nch), in_specs=in_specs, out_specs=[out_spec, out_spec],
        scratch_shapes=[pltpu.VMEM((2 * H, HEAD_W), BF16), pltpu.VMEM((H, HEAD_W), BF16),
                        pltpu.VMEM((2 * H, LANES), F32), pltpu.VMEM((2 * H, LANES), F32),
                        pltpu.VMEM((2 * H, HEAD_W), F32),
                        pltpu.VMEM((H, LANES), F32), pltpu.VMEM((H, LANES), F32), pltpu.VMEM((H, HEAD_W), F32),
                        pltpu.VMEM((1, LANES), F32)])
    return pl.pallas_call(
        functools.partial(_decode_kernel, n_pages=n_pages, lam0=lam0),
        grid_spec=grid_spec,
        out_shape=[jax.ShapeDtypeStruct((B, H, HEAD_W), BF16), jax.ShapeDtypeStruct((B, H, HEAD_W), BF16)],
        compiler_params=_cparams(("arbitrary", "arbitrary")),
        name="decode_attn",
    )(page_table, *args)


def _mem_decode_kernel(q_ref, mk_ref, mv_ref, o_ref):
    q = q_ref[...].astype(F32)
    s = jnp.sum(mk_ref[...] * q[None], axis=-1, keepdims=True)
    p = jnp.exp(s - jnp.max(s, axis=0, keepdims=True))
    o = jnp.sum(p * mv_ref[...], axis=0) / jnp.sum(p, axis=0)
    o_ref[...] = o.astype(o_ref.dtype)


def _mem_decode_attention(qb, cmk, cmv, l):
    B, W = qb.shape
    NM, HM = cmk.shape[2:4]
    out = pl.pallas_call(
        _mem_decode_kernel,
        grid=(B,),
        in_specs=[pl.BlockSpec((None, HM, HEAD_W), lambda b: (b, 0, 0)),
                  pl.BlockSpec((None, None, NM, HM, HEAD_W), lambda b: (l, b, 0, 0, 0)),
                  pl.BlockSpec((None, None, NM, HM, HEAD_W), lambda b: (l, b, 0, 0, 0))],
        out_specs=pl.BlockSpec((None, HM, HEAD_W), lambda b: (b, 0, 0)),
        out_shape=jax.ShapeDtypeStruct((B, HM, HEAD_W), BF16),
        compiler_params=_cparams(("arbitrary",)),
        name="mem_decode_attn",
    )(qb.reshape(B, HM, HEAD_W), cmk, cmv)
    return out.reshape(B, W)


def _bias_by_distance(rel_bias, dist):
    bucket = _rel_bucket(dist)
    rel = (rel_bias - rel_bias[N_BUCKETS - 1]) * LOG2E
    out = jnp.zeros((rel_bias.shape[1],) + dist.shape, F32)
    for b in range(N_BUCKETS - 1):
        out = jnp.where((bucket == b)[None], rel[b].reshape((-1,) + (1,) * dist.ndim), out)
    return out


def _bias_tables(rel_bias, blk):
    assert blk >= MAX_DISTANCE
    r = jnp.arange(blk, dtype=jnp.int32)[:, None]
    c = jnp.arange(blk, dtype=jnp.int32)[None, :]
    diag = jnp.where((r >= c)[None], _bias_by_distance(rel_bias, jnp.maximum(r - c, 0)), NEG)
    sub = _bias_by_distance(rel_bias, blk + r - c)
    return diag, sub


def _decode_bias(rel_bias, page_size):
    assert page_size >= MAX_DISTANCE
    last = _bias_by_distance(rel_bias, page_size - jnp.arange(page_size, dtype=jnp.int32))
    new = jnp.broadcast_to(_bias_by_distance(rel_bias, jnp.zeros((1,), jnp.int32)), (N_HEADS, LANES))
    return last.T.reshape(1, page_size * N_HEADS), jnp.concatenate([new, new], axis=0)


def kernel(x_prompt, x_sample, cache_diff_k, cache_diff_v, cache_fox_k, cache_fox_v, cache_fox_logf,
           cache_mem_k, cache_mem_v, page_table, mem_prompt, w_in, b_forget, diff_lambda, diff_subln_g,
           rel_bias, w_o, w_mq, w_mkv, w_mo, w_ff_in, w_ff_out, ln_g, ln_b):
    depth = w_in.shape[0]
    alpha = (2 * depth) ** 0.25
    _, T, D = x_prompt.shape
    DB = x_sample.shape[0]
    W = N_HEADS * HEAD_W
    n_qkv = 6 * W
    pool, page_size = cache_diff_k.shape[1:3]
    n_mem = mem_prompt.shape[1]
    w_mem = w_mq.shape[2]
    blk = min(ATT_BLK, T)
    blk_fox = 2 * blk if T % (2 * blk) == 0 else blk
    tm = min(512, T)
    assert rel_bias.shape == (N_BUCKETS, N_HEADS) and w_in.shape[2] == n_qkv + N_HEADS

    w_in_b = w_in.astype(BF16)
    w_fl_b = jnp.pad(w_in[:, :, n_qkv:], ((0, 0), (0, 0), (0, LANES - N_HEADS))).astype(BF16)
    b_fl = jnp.pad(b_forget, ((0, 0), (0, LANES - N_HEADS))).reshape(depth, 1, LANES)
    w_o_b, w_mq_b, w_mkv_b, w_mo_b = (w.astype(BF16) for w in (w_o, w_mq, w_mkv, w_mo))
    w_ff_in_b, w_ff_out_b = w_ff_in.astype(BF16), w_ff_out.astype(BF16)
    ln_g4 = ln_g.reshape(depth, 3, 1, D)
    ln_b4 = ln_b.reshape(depth, 3, 1, D)
    g_sub = diff_subln_g.reshape(depth, 1, HEAD_W)
    qk_scales = (DH_DIFF ** -0.5 * LOG2E, 1.0, 1.0, HEAD_W ** -0.5 * LOG2E, 1.0, 1.0)
    bias_diag, bias_sub = _bias_tables(rel_bias, blk)
    bias_last, bias_new = _decode_bias(rel_bias, page_size)
    d_ff = w_ff_out.shape[1]
    tn_ff = 512 if d_ff % 512 == 0 else d_ff
    tk_ff = d_ff // 2 if d_ff % (2 * LANES) == 0 else d_ff
    tm_wide = min(T, 2 * tm)

    def ffn(xf, xb, l, tmm, tm_up):
        hff = _swiglu_up(xb, w_ff_in_b, l, tm_up, tn_ff)
        return _mm_post_norm([hff], w_ff_out_b, l, xf, ln_g4, ln_b4, 2, alpha, tmm, tk_ff)

    xf = x_prompt.reshape(T, D)
    xb = xf.astype(BF16)
    memb = mem_prompt.reshape(n_mem, D).astype(BF16)
    p_kv, p_lf, p_mk, p_mv = None, [], [], []
    for l in range(depth):
        lam0 = _lambda_init(l)
        *p_kv, zb = _in_proj(xb, w_in_b, l, tm_wide, qk_scales, p_kv)
        lf, ct = _logits(xb, w_fl_b, b_fl, l, tm, True)
        od = _diff_attention(zb, bias_diag, bias_sub, diff_lambda, g_sub, l, lam0, blk, bf16_exp=True)
        of = _fox_attention(zb, ct, blk_fox, bf16_exp=False)
        xf, xb = _mm_post_norm([od, of], w_o_b, l, xf, ln_g4, ln_b4, 0, alpha, tm)
        mkv_f, mkv_b = _proj(memb, w_mkv_b, l, w_mem, n_mem, want_f32=True)
        xf, xb = _mem_attention(xb, w_mq_b, mkv_b[:, :w_mem], mkv_b[:, w_mem:], w_mo_b, l, xf, ln_g4, ln_b4, alpha, tm)
        xf, xb = ffn(xf, xb, l, tm, tm_wide)
        p_lf.append(lf[:, :N_HEADS])
        p_mk.append(mkv_f[:, :w_mem])
        p_mv.append(mkv_f[:, w_mem:])
    y_prompt = xf.reshape(1, T, D)

    assert page_size * N_HEADS == 8 * LANES
    caches = (cache_diff_k, cache_diff_v, cache_fox_k, cache_fox_v, cache_fox_logf.reshape(depth, pool, 8, LANES))
    per_seq = page_table.shape[1]
    n_pages = 8 if per_seq % 8 == 0 else per_seq
    xf = x_sample.reshape(DB, D)
    xb = xf.astype(BF16)
    s_kv, s_lf = None, []
    for l in range(depth):
        lam0 = _lambda_init(l)
        *s_kv, zb = _in_proj(xb, w_in_b, l, DB, qk_scales, s_kv)
        (lf,) = _logits(xb, w_fl_b, b_fl, l, DB, False)
        od, of = _decode_attention(page_table, jnp.swapaxes(zb, 0, 1), s_kv, lf, caches,
                                   bias_last, bias_new, diff_lambda, g_sub, l, lam0, n_pages)
        xf, xb = _mm_post_norm([od.reshape(DB, W), of.reshape(DB, W)], w_o_b, l, xf, ln_g4, ln_b4, 0, alpha, DB)
        (qm,) = _proj(xb, w_mq_b, l, w_mem, DB, scale=HEAD_W ** -0.5)
        om = _mem_decode_attention(qm, cache_mem_k, cache_mem_v, l)
        xf, xb = _mm_post_norm([om], w_mo_b, l, xf, ln_g4, ln_b4, 1, alpha, DB)
        xf, xb = ffn(xf, xb, l, DB, DB)
        s_lf.append(lf[:, :N_HEADS])
    y_sample = xf.reshape(DB, 1, D)

    def st(parts, *shape):
        return jnp.stack(parts).reshape(depth, *shape)

    return (y_prompt, y_sample,
            *(a.reshape(depth, 1, T, N_HEADS, HEAD_W) for a in p_kv), st(p_lf, 1, T, N_HEADS),
            st(p_mk, 1, n_mem, N_HEADS_MEM, HEAD_W), st(p_mv, 1, n_mem, N_HEADS_MEM, HEAD_W),
            *(a.reshape(depth, DB, 1, N_HEADS, HEAD_W) for a in s_kv), st(s_lf, DB, 1, N_HEADS))
```
